```python
import math
import jax, jax.numpy as jnp
from jax import lax
import numpy as np

D_MODEL = 1024
BATCH = 8
SEQ = 4096
DEPTH = 1

D_MIX = D_MODEL
HEAD_DIM = 64
ATTN_HEADS = 8
ATTN_WIDTH = ATTN_HEADS * HEAD_DIM
DILATED_PATTERNS = ((128, 1), (512, 4), (2048, 16))
ROPE_THETA = 10000.0
SSD_HEADS = 8
SSD_HEAD_DIM = 64
SSD_WIDTH = SSD_HEADS * SSD_HEAD_DIM
SSD_GROUPS = 2
SSD_STATE = 128
SSD_CONV = 5
SSD_CHUNK = 128
SSD_CONV_CH = SSD_WIDTH + 2 * SSD_GROUPS * SSD_STATE
IN_PROJ_COLS = 3 * ATTN_WIDTH + SSD_WIDTH + SSD_CONV_CH + 2 * SSD_HEADS
N_EXPERTS = 32
TOP_K = 4
D_EXPERT = D_MODEL
SWIGLU_ALPHA = 1.702
SWIGLU_LIMIT = 7.0
DEEPNORM_ALPHA = (2.0 * DEPTH) ** 0.25
DEEPNORM_BETA = (8.0 * DEPTH) ** -0.25
NORM_EPS = 1e-5

kernel_name = "hymba_longnet_ssd_moe_deepnorm_encoder"


def layer_norm(x, w, b):
    xf = x.astype(jnp.float32)
    mu = jnp.mean(xf, axis=-1, keepdims=True)
    var = jnp.mean(jnp.square(xf - mu), axis=-1, keepdims=True)
    return (xf - mu) * lax.rsqrt(var + NORM_EPS) * w.astype(jnp.float32) + b.astype(jnp.float32)


def rms_norm(x, w):
    xf = x.astype(jnp.float32)
    return xf * lax.rsqrt(jnp.mean(jnp.square(xf), axis=-1, keepdims=True) + NORM_EPS) * w.astype(jnp.float32)


def rope_tables(s):
    pos = jnp.arange(s, dtype=jnp.float32)
    inv_freq = ROPE_THETA ** (-jnp.arange(0, HEAD_DIM, 2, dtype=jnp.float32) / HEAD_DIM)
    ang = pos[:, None] * inv_freq[None, :]
    return jnp.cos(ang), jnp.sin(ang)


def apply_rope(t, cos, sin):
    t1, t2 = jnp.split(t, 2, axis=-1)
    c = cos[None, :, None, :]
    s = sin[None, :, None, :]
    return jnp.concatenate([t1 * c - t2 * s, t2 * c + t1 * s], axis=-1)


def banded_window_attention(q, k, v, half):
    b, g, L, h, dh = q.shape
    blk = half
    nb = -(-L // blk)
    lp = nb * blk
    q = jnp.pad(q, ((0, 0), (0, 0), (0, lp - L), (0, 0), (0, 0)))
    kv_pad = ((0, 0), (0, 0), (blk, lp - L + blk), (0, 0), (0, 0))
    kb = jnp.pad(k, kv_pad).reshape(b, g, nb + 2, blk, h, dh)
    vb = jnp.pad(v, kv_pad).reshape(b, g, nb + 2, blk, h, dh)
    k_win = jnp.concatenate([kb[:, :, 0:nb], kb[:, :, 1:nb + 1], kb[:, :, 2:nb + 2]], axis=3)
    v_win = jnp.concatenate([vb[:, :, 0:nb], vb[:, :, 1:nb + 1], vb[:, :, 2:nb + 2]], axis=3)
    qb = q.reshape(b, g, nb, blk, h, dh)
    scores = jnp.einsum("bgnqhd,bgnkhd->bgnhqk", qb, k_win) * (dh ** -0.5)
    blocks = jnp.arange(nb)[:, None, None]
    qpos = blocks * blk + jnp.arange(blk)[None, :, None]
    kpos = (blocks - 1) * blk + jnp.arange(3 * blk)[None, None, :]
    valid = (jnp.abs(qpos - kpos) <= half) & (kpos >= 0) & (kpos < L)
    scores = jnp.where(valid[None, None, :, None], scores, -jnp.inf)
    mx = jnp.max(scores, axis=-1)
    p = jnp.exp(scores - mx[..., None])
    den = jnp.sum(p, axis=-1)
    num = jnp.einsum("bgnhqk,bgnkhd->bgnqhd", p, v_win).reshape(b, g, lp, h, dh)[:, :, :L]
    den = den.transpose(0, 1, 2, 4, 3).reshape(b, g, lp, h)[:, :, :L]
    mx = mx.transpose(0, 1, 2, 4, 3).reshape(b, g, lp, h)[:, :, :L]
    return num, den, mx


def dilated_attention(q, k, v):
    b, s, h, dh = q.shape
    nums, dens, mxs = [], [], []
    for window, dil in DILATED_PATTERNS:
        half = window // (2 * dil)
        L = s // dil
        qr = q.reshape(b, L, dil, h, dh).transpose(0, 2, 1, 3, 4)
        kr = k.reshape(b, L, dil, h, dh).transpose(0, 2, 1, 3, 4)
        vr = v.reshape(b, L, dil, h, dh).transpose(0, 2, 1, 3, 4)
        num, den, mx = banded_window_attention(qr, kr, vr, half)
        nums.append(num.transpose(0, 2, 1, 3, 4).reshape(b, s, h, dh))
        dens.append(den.transpose(0, 2, 1, 3).reshape(b, s, h))
        mxs.append(mx.transpose(0, 2, 1, 3).reshape(b, s, h))
    num_all = jnp.stack(nums)
    den_all = jnp.stack(dens)
    mx_all = jnp.stack(mxs)
    w = jnp.exp(mx_all - jnp.max(mx_all, axis=0, keepdims=True))
    return jnp.sum(w[..., None] * num_all, axis=0) / jnp.sum(w * den_all, axis=0)[..., None]


def ssd_chunked(xs, a, bm, cm):
    b, l, h, p = xs.shape
    n = bm.shape[-1]
    t = SSD_CHUNK
    c = l // t
    xs = xs.reshape(b, c, t, h, p)
    bm = bm.reshape(b, c, t, h, n)
    cm = cm.reshape(b, c, t, h, n)
    a = a.reshape(b, c, t, h).transpose(0, 3, 1, 2)
    a_cs = jnp.cumsum(a, axis=-1)
    seg = a_cs[..., :, None] - a_cs[..., None, :]
    tril = jnp.tril(jnp.ones((t, t), dtype=bool))
    lmat = jnp.exp(jnp.where(tril, seg, -jnp.inf))
    cb = jnp.einsum("bclhn,bcshn->bhcls", cm, bm)
    y_diag = jnp.einsum("bhcls,bcshp->bclhp", cb * lmat, xs)
    decay_states = jnp.exp(a_cs[..., -1:] - a_cs)
    states = jnp.einsum("bclhn,bhcl,bclhp->bchpn", bm, decay_states, xs)
    chunk_decay = jnp.exp(a_cs[..., -1])

    def step(state, inp):
        st, dec = inp
        return state * dec[..., None, None] + st, state

    init = jnp.zeros((b, h, p, n), dtype=jnp.float32)
    _, prev = lax.scan(step, init, (states.transpose(1, 0, 2, 3, 4), chunk_decay.transpose(2, 0, 1)))
    prev = prev.transpose(1, 0, 2, 3, 4)
    y_off = jnp.einsum("bclhn,bchpn,bhcl->bclhp", cm, prev, jnp.exp(a_cs))
    return (y_diag + y_off).reshape(b, l, h, p)


def bidirectional_ssd(z, xbc, dt_raw, conv_w, conv_b, dt_bias_fwd, a_log_fwd,
                      dt_bias_bwd, a_log_bwd, d_skip, norm_w):
    b, s, _ = z.shape
    xbc = lax.conv_general_dilated(
        xbc.astype(jnp.float32), conv_w.astype(jnp.float32)[:, None, :], (1,),
        [(SSD_CONV // 2, SSD_CONV // 2)], dimension_numbers=("NWC", "WIO", "NWC"),
        feature_group_count=SSD_CONV_CH) + conv_b.astype(jnp.float32)
    xbc = jax.nn.silu(xbc)
    gn = SSD_GROUPS * SSD_STATE
    xs = xbc[..., :SSD_WIDTH].reshape(b, s, SSD_HEADS, SSD_HEAD_DIM)
    bm = xbc[..., SSD_WIDTH:SSD_WIDTH + gn].reshape(b, s, SSD_GROUPS, SSD_STATE)
    cm = xbc[..., SSD_WIDTH + gn:].reshape(b, s, SSD_GROUPS, SSD_STATE)
    heads_per_group = SSD_HEADS // SSD_GROUPS
    bm = jnp.repeat(bm, heads_per_group, axis=2)
    cm = jnp.repeat(cm, heads_per_group, axis=2)
    dt_raw = dt_raw.astype(jnp.float32)
    dt_f = jax.nn.softplus(dt_raw[..., :SSD_HEADS] + dt_bias_fwd.astype(jnp.float32))
    dt_b = jax.nn.softplus(dt_raw[..., SSD_HEADS:] + dt_bias_bwd.astype(jnp.float32))
    a_f = -jnp.exp(a_log_fwd.astype(jnp.float32))
    a_b = -jnp.exp(a_log_bwd.astype(jnp.float32))
    y_f = ssd_chunked(xs * dt_f[..., None], dt_f * a_f, bm, cm)
    flip = lambda t: jnp.flip(t, axis=1)
    y_b = flip(ssd_chunked(flip(xs * dt_b[..., None]), flip(dt_b * a_b), flip(bm), flip(cm)))
    y = y_f + y_b + d_skip.astype(jnp.float32)[:, None] * xs
    y = y.reshape(b, s, SSD_WIDTH) * jax.nn.silu(z.astype(jnp.float32))
    yg = y.reshape(b, s, SSD_GROUPS, SSD_WIDTH // SSD_GROUPS)
    yg = yg * lax.rsqrt(jnp.mean(jnp.square(yg), axis=-1, keepdims=True) + NORM_EPS)
    return yg.reshape(b, s, SSD_WIDTH) * norm_w.astype(jnp.float32)


def moe_ffn(x, router_w, router_b, w_gate, b_gate, w_up, b_up, w_down, b_down):
    b, s, d = x.shape
    xt = x.reshape(b * s, d)
    logits = (xt @ router_w + router_b).astype(jnp.float32)
    top_vals, top_idx = lax.top_k(logits, TOP_K)
    gates = jax.nn.softmax(top_vals, axis=-1)
    combine = jnp.sum(jax.nn.one_hot(top_idx, N_EXPERTS, dtype=jnp.float32) * gates[..., None], axis=1)
    out = jnp.zeros((b * s, d), dtype=jnp.float32)
    for e in range(N_EXPERTS):
        g = (xt @ w_gate[e] + b_gate[e]).astype(jnp.float32)
        u = (xt @ w_up[e] + b_up[e]).astype(jnp.float32)
        g = jnp.minimum(g, SWIGLU_LIMIT)
        u = jnp.clip(u, -SWIGLU_LIMIT, SWIGLU_LIMIT)
        act = (u + 1.0) * g * jax.nn.sigmoid(SWIGLU_ALPHA * g)
        y = act.astype(x.dtype) @ w_down[e] + b_down[e]
        out = out + combine[:, e:e + 1] * y.astype(jnp.float32)
    return out.reshape(b, s, d)


def setup_inputs(seed: int = 0) -> dict:
    key = jax.random.key(seed)
    ks = jax.random.split(key, 24)
    f32 = jnp.float32
    nrm = lambda k, shape, scale: jax.random.normal(k, shape, f32) * scale
    dt = jnp.exp(jax.random.uniform(ks[4], (DEPTH, SSD_HEADS), f32, math.log(1e-3), math.log(1e-1)))
    dt2 = jnp.exp(jax.random.uniform(ks[6], (DEPTH, SSD_HEADS), f32, math.log(1e-3), math.log(1e-1)))
    return {
        "x": jax.random.normal(ks[0], (BATCH, SEQ, D_MODEL), f32),
        "w_in": nrm(ks[1], (DEPTH, D_MODEL, IN_PROJ_COLS), D_MODEL ** -0.5),
        "attn_norm_w": 1.0 + nrm(ks[2], (DEPTH, ATTN_WIDTH), 0.02),
        "conv_w": nrm(ks[3], (DEPTH, SSD_CONV, SSD_CONV_CH), SSD_CONV ** -0.5),
        "conv_b": nrm(ks[21], (DEPTH, SSD_CONV_CH), 0.02),
        "dt_bias_fwd": dt + jnp.log(-jnp.expm1(-dt)),
        "a_log_fwd": jnp.log(jax.random.uniform(ks[5], (DEPTH, SSD_HEADS), f32, 1.0, 16.0)),
        "dt_bias_bwd": dt2 + jnp.log(-jnp.expm1(-dt2)),
        "a_log_bwd": jnp.log(jax.random.uniform(ks[7], (DEPTH, SSD_HEADS), f32, 1.0, 16.0)),
        "d_skip": 1.0 + nrm(ks[8], (DEPTH, SSD_HEADS), 0.1),
        "ssd_norm_w": 1.0 + nrm(ks[9], (DEPTH, SSD_WIDTH), 0.02),
        "w_out": nrm(ks[10], (DEPTH, D_MIX, D_MODEL), DEEPNORM_BETA * D_MIX ** -0.5),
        "ln1_w": 1.0 + nrm(ks[11], (DEPTH, D_MODEL), 0.02),
        "ln1_b": nrm(ks[12], (DEPTH, D_MODEL), 0.02),
        "router_w": nrm(ks[13], (DEPTH, D_MODEL, N_EXPERTS), D_MODEL ** -0.5),
        "router_b": nrm(ks[14], (DEPTH, N_EXPERTS), 0.01),
        "w_gate": nrm(ks[15], (DEPTH, N_EXPERTS, D_MODEL, D_EXPERT), D_MODEL ** -0.5),
        "b_gate": nrm(ks[16], (DEPTH, N_EXPERTS, D_EXPERT), 0.02),
        "w_up": nrm(ks[17], (DEPTH, N_EXPERTS, D_MODEL, D_EXPERT), D_MODEL ** -0.5),
        "b_up": nrm(ks[18], (DEPTH, N_EXPERTS, D_EXPERT), 0.02),
        "w_down": nrm(ks[19], (DEPTH, N_EXPERTS, D_EXPERT, D_MODEL), DEEPNORM_BETA * D_EXPERT ** -0.5),
        "b_down": nrm(ks[20], (DEPTH, N_EXPERTS, D_MODEL), 0.02),
        "ln2_w": 1.0 + nrm(ks[22], (DEPTH, D_MODEL), 0.02),
        "ln2_b": nrm(ks[23], (DEPTH, D_MODEL), 0.02),
    }


def reference(x, w_in, attn_norm_w, conv_w, conv_b, dt_bias_fwd, a_log_fwd, dt_bias_bwd,
              a_log_bwd, d_skip, ssd_norm_w, w_out, ln1_w, ln1_b, router_w, router_b,
              w_gate, b_gate, w_up, b_up, w_down, b_down, ln2_w, ln2_b):
    b, s, _ = x.shape
    cos, sin = rope_tables(s)
    h = x
    q_end = ATTN_WIDTH
    k_end = 2 * ATTN_WIDTH
    v_end = 3 * ATTN_WIDTH
    z_end = v_end + SSD_WIDTH
    xbc_end = z_end + SSD_CONV_CH
    for layer in range(DEPTH):
        proj = h @ w_in[layer]
        q = proj[..., :q_end].reshape(b, s, ATTN_HEADS, HEAD_DIM).astype(jnp.float32)
        k = proj[..., q_end:k_end].reshape(b, s, ATTN_HEADS, HEAD_DIM).astype(jnp.float32)
        v = proj[..., k_end:v_end].reshape(b, s, ATTN_HEADS, HEAD_DIM).astype(jnp.float32)
        attn = dilated_attention(apply_rope(q, cos, sin), apply_rope(k, cos, sin), v)
        attn = rms_norm(attn.reshape(b, s, ATTN_WIDTH), attn_norm_w[layer])
        ssd = bidirectional_ssd(proj[..., v_end:z_end], proj[..., z_end:xbc_end], proj[..., xbc_end:],
                                conv_w[layer], conv_b[layer], dt_bias_fwd[layer], a_log_fwd[layer],
                                dt_bias_bwd[layer], a_log_bwd[layer], d_skip[layer], ssd_norm_w[layer])
        mix = jnp.concatenate([attn, ssd], axis=-1).astype(h.dtype) @ w_out[layer]
        h1 = layer_norm(DEEPNORM_ALPHA * h.astype(jnp.float32) + mix.astype(jnp.float32),
                        ln1_w[layer], ln1_b[layer]).astype(x.dtype)
        ffn = moe_ffn(h1, router_w[layer], router_b[layer], w_gate[layer], b_gate[layer],
                      w_up[layer], b_up[layer], w_down[layer], b_down[layer])
        h = layer_norm(DEEPNORM_ALPHA * h1.astype(jnp.float32) + ffn,
                       ln2_w[layer], ln2_b[layer]).astype(x.dtype)
    return h
```

```python
import functools
import math

import jax
import jax.numpy as jnp
from jax import lax
from jax.experimental import pallas as pl
from jax.experimental.pallas import tpu as pltpu

D_MODEL = 1024
HEAD_DIM = 64
ATTN_HEADS = 8
ATTN_WIDTH = ATTN_HEADS * HEAD_DIM
DILATED_PATTERNS = ((128, 1), (512, 4), (2048, 16))
ROPE_THETA = 10000.0
SSD_HEADS = 8
SSD_HEAD_DIM = 64
SSD_WIDTH = SSD_HEADS * SSD_HEAD_DIM
SSD_GROUPS = 2
SSD_STATE = 128
SSD_CONV = 5
SSD_CHUNK = 128
SSD_CONV_CH = SSD_WIDTH + 2 * SSD_GROUPS * SSD_STATE
N_EXPERTS = 32
TOP_K = 4
SWIGLU_ALPHA = 1.702
SWIGLU_LIMIT = 7.0
DEPTH = 1
DEEPNORM_ALPHA = (2.0 * DEPTH) ** 0.25
NORM_EPS = 1e-5

V7X_VMEM_BYTES = 64 * 1024 * 1024
LANES = 128

MATMUL_ROWS = 512
MOE_TILE = 512
FFN_CHUNK = 256
LN_ROWS = 512


def _rows_matmul_kernel(x_ref, w_ref, o_ref):
    o_ref[...] = jnp.dot(x_ref[...].astype(jnp.bfloat16), w_ref[...],
                         preferred_element_type=jnp.float32)


def _rows_matmul(x, w_bf16, name):
    n, k = x.shape
    _, m = w_bf16.shape
    assert n % MATMUL_ROWS == 0 and m % LANES == 0
    return pl.pallas_call(
        _rows_matmul_kernel,
        grid=(n // MATMUL_ROWS,),
        in_specs=[pl.BlockSpec((MATMUL_ROWS, k), lambda i: (i, 0)),
                  pl.BlockSpec((k, m), lambda i: (0, 0))],
        out_specs=pl.BlockSpec((MATMUL_ROWS, m), lambda i: (i, 0)),
        out_shape=jax.ShapeDtypeStruct((n, m), jnp.float32),
        compiler_params=pltpu.CompilerParams(
            dimension_semantics=("arbitrary",),
            vmem_limit_bytes=48 * 1024 * 1024),
        name=name,
    )(x, w_bf16)


def _moe_kernel(te_ref, nt_ref,
                src_cur_ref, src_nxt_ref, dst_ref,
                h_hbm, wg_ref, wu_ref, wd_ref, bg_ref, bu_ref, bd_ref,
                y_hbm,
                xbuf, ybuf, wg_bf, wu_bf, wd_bf, gsem, ssem):
    i = pl.program_id(0)
    n_tiles = nt_ref[0]
    slot = i % 2

    def gather_copy(tok, row, s):
        return pltpu.make_async_copy(h_hbm.at[pl.ds(tok, 1), :],
                                     xbuf.at[s, pl.ds(row, 1), :], gsem.at[s])

    def scatter_copy(dst, row, s):
        return pltpu.make_async_copy(ybuf.at[s, pl.ds(row, 1), :],
                                     y_hbm.at[pl.ds(dst, 1), :], ssem.at[s])

    def start_gather(idx_ref, s):
        def body(r, c):
            gather_copy(idx_ref[0, 0, r], r, s).start()
            return c
        lax.fori_loop(0, MOE_TILE, body, 0)

    def wait_gather(s):
        def body(r, c):
            gather_copy(0, r, s).wait()
            return c
        lax.fori_loop(0, MOE_TILE, body, 0)

    def wait_scatter(s):
        def body(r, c):
            scatter_copy(0, r, s).wait()
            return c
        lax.fori_loop(0, MOE_TILE, body, 0)

    @pl.when(i == 0)
    def _init_spare():
        ybuf[0] = jnp.zeros((MOE_TILE, D_MODEL), jnp.float32)
        n_pairs = y_hbm.shape[0] - 2 * MOE_TILE
        for half in range(2):
            cp = pltpu.make_async_copy(
                ybuf.at[0], y_hbm.at[pl.ds(n_pairs + half * MOE_TILE, MOE_TILE), :], ssem.at[half])
            cp.start()
            cp.wait()

    @pl.when(i < n_tiles)
    def _active():
        @pl.when(i == 0)
        def _():
            start_gather(src_cur_ref, 0)

        wait_gather(slot)

        @pl.when(i + 1 < n_tiles)
        def _():
            start_gather(src_nxt_ref, 1 - slot)

        e = te_ref[i]
        prev_e = te_ref[jnp.maximum(i - 1, 0)]

        @pl.when((i == 0) | (e != prev_e))
        def _():
            wg_bf[...] = wg_ref[0].astype(jnp.bfloat16)
            wu_bf[...] = wu_ref[0].astype(jnp.bfloat16)
            wd_bf[...] = wd_ref[0].astype(jnp.bfloat16)

        xb = xbuf[slot].astype(jnp.bfloat16)
        acc = jnp.zeros((MOE_TILE, D_MODEL), jnp.float32)
        for c in range(D_MODEL // FFN_CHUNK):
            cols = slice(c * FFN_CHUNK, (c + 1) * FFN_CHUNK)
            g = jnp.dot(xb, wg_bf[:, cols], preferred_element_type=jnp.float32)
            g = g + bg_ref[pl.ds(e, 1), cols]
            u = jnp.dot(xb, wu_bf[:, cols], preferred_element_type=jnp.float32)
            u = u + bu_ref[pl.ds(e, 1), cols]
            g = jnp.minimum(g, SWIGLU_LIMIT)
            u = jnp.clip(u, -SWIGLU_LIMIT, SWIGLU_LIMIT)
            act = (u + 1.0) * g * jax.nn.sigmoid(SWIGLU_ALPHA * g)
            acc = acc + jnp.dot(act.astype(jnp.bfloat16), wd_bf[cols, :],
                                preferred_element_type=jnp.float32)
        y = acc + bd_ref[pl.ds(e, 1), :]

        @pl.when(i >= 2)
        def _():
            wait_scatter(slot)

        ybuf[slot] = y

        def sbody(r, c):
            scatter_copy(dst_ref[0, 0, r], r, slot).start()
            return c
        lax.fori_loop(0, MOE_TILE, sbody, 0)

        @pl.when(i == n_tiles - 1)
        def _():
            @pl.when(i >= 1)
            def _():
                wait_scatter(1 - slot)
            wait_scatter(slot)


def _moe_ffn(h1, tile_e, n_tiles, src_tok, dst_row, w_gate, b_gate, w_up, b_up,
             w_down, b_down, n_out_rows):
    t_max = tile_e.shape[0]
    d = D_MODEL
    idx_cur = pl.BlockSpec((1, 1, MOE_TILE), lambda i, te, nt: (i, 0, 0),
                           memory_space=pltpu.SMEM)
    idx_nxt = pl.BlockSpec((1, 1, MOE_TILE),
                           lambda i, te, nt: (jnp.minimum(i + 1, t_max - 1), 0, 0),
                           memory_space=pltpu.SMEM)
    w_spec = pl.BlockSpec((1, d, d), lambda i, te, nt: (te[i], 0, 0))
    b_spec = pl.BlockSpec((N_EXPERTS, d), lambda i, te, nt: (0, 0))
    grid_spec = pltpu.PrefetchScalarGridSpec(
        num_scalar_prefetch=2,
        grid=(t_max,),
        in_specs=[idx_cur, idx_nxt, idx_cur,
                  pl.BlockSpec(memory_space=pl.ANY),
                  w_spec, w_spec, w_spec, b_spec, b_spec, b_spec],
        out_specs=pl.BlockSpec(memory_space=pl.ANY),
        scratch_shapes=[
            pltpu.VMEM((2, MOE_TILE, d), jnp.float32),
            pltpu.VMEM((2, MOE_TILE, d), jnp.float32),
            pltpu.VMEM((d, d), jnp.bfloat16),
            pltpu.VMEM((d, d), jnp.bfloat16),
            pltpu.VMEM((d, d), jnp.bfloat16),
            pltpu.SemaphoreType.DMA((2,)),
            pltpu.SemaphoreType.DMA((2,)),
        ],
    )
    return pl.pallas_call(
        _moe_kernel,
        grid_spec=grid_spec,
        out_shape=jax.ShapeDtypeStruct((n_out_rows, d), jnp.float32),
        compiler_params=pltpu.CompilerParams(
            dimension_semantics=("arbitrary",),
            vmem_limit_bytes=V7X_VMEM_BYTES - 6 * 1024 * 1024),
        name="moe_ffn",
    )(tile_e, n_tiles, src_tok, src_tok, dst_row, h1,
      w_gate, w_up, w_down, b_gate, b_up, b_down)


def _combine_ln_kernel(y0_ref, y1_ref, y2_ref, y3_ref, h_ref, g_ref, w_ref, b_ref, o_ref):
    g = g_ref[...]
    ffn = (g[:, 0:1] * y0_ref[...] + g[:, 1:2] * y1_ref[...]
           + g[:, 2:3] * y2_ref[...] + g[:, 3:4] * y3_ref[...])
    v = DEEPNORM_ALPHA * h_ref[...] + ffn
    mu = jnp.mean(v, axis=-1, keepdims=True)
    var = jnp.mean(jnp.square(v - mu), axis=-1, keepdims=True)
    o_ref[...] = (v - mu) * lax.rsqrt(var + NORM_EPS) * w_ref[...] + b_ref[...]


def _combine_ln(y_planes, h1, gates_pad, ln_w, ln_b):
    n, d = h1.shape
    nb = n // LN_ROWS

    def plane(k):
        return pl.BlockSpec((LN_ROWS, d), lambda i, k=k: (k * nb + i, 0))

    row = pl.BlockSpec((LN_ROWS, d), lambda i: (i, 0))
    vec = pl.BlockSpec((1, d), lambda i: (0, 0))
    return pl.pallas_call(
        _combine_ln_kernel,
        grid=(nb,),
        in_specs=[plane(0), plane(1), plane(2), plane(3), row,
                  pl.BlockSpec((LN_ROWS, LANES), lambda i: (i, 0)), vec, vec],
        out_specs=row,
        out_shape=jax.ShapeDtypeStruct((n, d), jnp.float32),
        compiler_params=pltpu.CompilerParams(
            dimension_semantics=("arbitrary",),
            vmem_limit_bytes=48 * 1024 * 1024),
        name="combine_ln",
    )(y_planes, y_planes, y_planes, y_planes, h1, gates_pad,
      ln_w.reshape(1, d), ln_b.reshape(1, d))


def _layer_norm(x, w, b):
    mu = jnp.mean(x, axis=-1, keepdims=True)
    var = jnp.mean(jnp.square(x - mu), axis=-1, keepdims=True)
    return (x - mu) * lax.rsqrt(var + NORM_EPS) * w + b


def _rms_norm(x, w):
    return x * lax.rsqrt(jnp.mean(jnp.square(x), axis=-1, keepdims=True) + NORM_EPS) * w


def _rope_tables(s):
    pos = jnp.arange(s, dtype=jnp.float32)
    inv_freq = ROPE_THETA ** (-jnp.arange(0, HEAD_DIM, 2, dtype=jnp.float32) / HEAD_DIM)
    ang = pos[:, None] * inv_freq[None, :]
    return jnp.cos(ang), jnp.sin(ang)


def _apply_rope(t, cos, sin):
    t1, t2 = jnp.split(t, 2, axis=-1)
    c = cos[None, :, None, :]
    s = sin[None, :, None, :]
    return jnp.concatenate([t1 * c - t2 * s, t2 * c + t1 * s], axis=-1)


def _banded_window_attention(q, k, v, half):
    b, g, L, h, dh = q.shape
    blk = half
    nb = -(-L // blk)
    lp = nb * blk
    q = jnp.pad(q, ((0, 0), (0, 0), (0, lp - L), (0, 0), (0, 0)))
    kv_pad = ((0, 0), (0, 0), (blk, lp - L + blk), (0, 0), (0, 0))
    kb = jnp.pad(k, kv_pad).reshape(b, g, nb + 2, blk, h, dh)
    vb = jnp.pad(v, kv_pad).reshape(b, g, nb + 2, blk, h, dh)
    k_win = jnp.concatenate([kb[:, :, 0:nb], kb[:, :, 1:nb + 1], kb[:, :, 2:nb + 2]], axis=3)
    v_win = jnp.concatenate([vb[:, :, 0:nb], vb[:, :, 1:nb + 1], vb[:, :, 2:nb + 2]], axis=3)
    qb = q.reshape(b, g, nb, blk, h, dh)
    scores = jnp.einsum("bgnqhd,bgnkhd->bgnhqk", qb, k_win) * (dh ** -0.5)
    blocks = jnp.arange(nb)[:, None, None]
    qpos = blocks * blk + jnp.arange(blk)[None, :, None]
    kpos = (blocks - 1) * blk + jnp.arange(3 * blk)[None, None, :]
    valid = (jnp.abs(qpos - kpos) <= half) & (kpos >= 0) & (kpos < L)
    scores = jnp.where(valid[None, None, :, None], scores, -jnp.inf)
    mx = jnp.max(scores, axis=-1)
    p = jnp.exp(scores - mx[..., None])
    den = jnp.sum(p, axis=-1)
    num = jnp.einsum("bgnhqk,bgnkhd->bgnqhd", p, v_win).reshape(b, g, lp, h, dh)[:, :, :L]
    den = den.transpose(0, 1, 2, 4, 3).reshape(b, g, lp, h)[:, :, :L]
    mx = mx.transpose(0, 1, 2, 4, 3).reshape(b, g, lp, h)[:, :, :L]
    return num, den, mx


def _dilated_attention(q, k, v):
    b, s, h, dh = q.shape
    nums, dens, mxs = [], [], []
    for window, dil in DILATED_PATTERNS:
        half = window // (2 * dil)
        L = s // dil
        qr = q.reshape(b, L, dil, h, dh).transpose(0, 2, 1, 3, 4)
        kr = k.reshape(b, L, dil, h, dh).transpose(0, 2, 1, 3, 4)
        vr = v.reshape(b, L, dil, h, dh).transpose(0, 2, 1, 3, 4)
        num, den, mx = _banded_window_attention(qr, kr, vr, half)
        nums.append(num.transpose(0, 2, 1, 3, 4).reshape(b, s, h, dh))
        dens.append(den.transpose(0, 2, 1, 3).reshape(b, s, h))
        mxs.append(mx.transpose(0, 2, 1, 3).reshape(b, s, h))
    num_all = jnp.stack(nums)
    den_all = jnp.stack(dens)
    mx_all = jnp.stack(mxs)
    w = jnp.exp(mx_all - jnp.max(mx_all, axis=0, keepdims=True))
    return jnp.sum(w[..., None] * num_all, axis=0) / jnp.sum(w * den_all, axis=0)[..., None]


def _ssd_chunked(xs, a, bm, cm):
    b, l, h, p = xs.shape
    n = bm.shape[-1]
    t = SSD_CHUNK
    c = l // t
    xs = xs.reshape(b, c, t, h, p)
    bm = bm.reshape(b, c, t, h, n)
    cm = cm.reshape(b, c, t, h, n)
    a = a.reshape(b, c, t, h).transpose(0, 3, 1, 2)
    a_cs = jnp.cumsum(a, axis=-1)
    seg = a_cs[..., :, None] - a_cs[..., None, :]
    tril = jnp.tril(jnp.ones((t, t), dtype=bool))
    lmat = jnp.exp(jnp.where(tril, seg, -jnp.inf))
    cb = jnp.einsum("bclhn,bcshn->bhcls", cm, bm)
    y_diag = jnp.einsum("bhcls,bcshp->bclhp", cb * lmat, xs)
    decay_states = jnp.exp(a_cs[..., -1:] - a_cs)
    states = jnp.einsum("bclhn,bhcl,bclhp->bchpn", bm, decay_states, xs)
    chunk_decay = jnp.exp(a_cs[..., -1])

    def step(state, inp):
        st, dec = inp
        return state * dec[..., None, None] + st, state

    init = jnp.zeros((b, h, p, n), dtype=jnp.float32)
    _, prev = lax.scan(step, init, (states.transpose(1, 0, 2, 3, 4), chunk_decay.transpose(2, 0, 1)))
    prev = prev.transpose(1, 0, 2, 3, 4)
    y_off = jnp.einsum("bclhn,bchpn,bhcl->bclhp", cm, prev, jnp.exp(a_cs))
    return (y_diag + y_off).reshape(b, l, h, p)


def _bidirectional_ssd(z, xbc, dt_raw, conv_w, conv_b, dt_bias_fwd, a_log_fwd,
                       dt_bias_bwd, a_log_bwd, d_skip, norm_w):
    b, s, _ = z.shape
    xbc = lax.conv_general_dilated(
        xbc, conv_w[:, None, :], (1,),
        [(SSD_CONV // 2, SSD_CONV // 2)], dimension_numbers=("NWC", "WIO", "NWC"),
        feature_group_count=SSD_CONV_CH) + conv_b
    xbc = jax.nn.silu(xbc)
    gn = SSD_GROUPS * SSD_STATE
    xs = xbc[..., :SSD_WIDTH].reshape(b, s, SSD_HEADS, SSD_HEAD_DIM)
    bm = xbc[..., SSD_WIDTH:SSD_WIDTH + gn].reshape(b, s, SSD_GROUPS, SSD_STATE)
    cm = xbc[..., SSD_WIDTH + gn:].reshape(b, s, SSD_GROUPS, SSD_STATE)
    heads_per_group = SSD_HEADS // SSD_GROUPS
    bm = jnp.repeat(bm, heads_per_group, axis=2)
    cm = jnp.repeat(cm, heads_per_group, axis=2)
    dt_f = jax.nn.softplus(dt_raw[..., :SSD_HEADS] + dt_bias_fwd)
    dt_b = jax.nn.softplus(dt_raw[..., SSD_HEADS:] + dt_bias_bwd)
    a_f = -jnp.exp(a_log_fwd)
    a_b = -jnp.exp(a_log_bwd)
    y_f = _ssd_chunked(xs * dt_f[..., None], dt_f * a_f, bm, cm)
    flip = lambda t: jnp.flip(t, axis=1)
    y_b = flip(_ssd_chunked(flip(xs * dt_b[..., None]), flip(dt_b * a_b), flip(bm), flip(cm)))
    y = y_f + y_b + d_skip[:, None] * xs
    y = y.reshape(b, s, SSD_WIDTH) * jax.nn.silu(z)
    yg = y.reshape(b, s, SSD_GROUPS, SSD_WIDTH // SSD_GROUPS)
    yg = yg * lax.rsqrt(jnp.mean(jnp.square(yg), axis=-1, keepdims=True) + NORM_EPS)
    return yg.reshape(b, s, SSD_WIDTH) * norm_w


def _routing_tables(top_idx, n_tok):
    n_pairs = TOP_K * n_tok
    t_max = n_pairs // MOE_TILE + N_EXPERTS
    flat_e = top_idx.T.reshape(-1).astype(jnp.int32)
    order = jnp.argsort(flat_e, stable=True).astype(jnp.int32)
    counts = jnp.sum((flat_e[:, None] == jnp.arange(N_EXPERTS, dtype=jnp.int32)[None, :])
                     .astype(jnp.int32), axis=0)
    tiles_per_e = (counts + MOE_TILE - 1) // MOE_TILE
    tile_end = jnp.cumsum(tiles_per_e)
    tile_start = tile_end - tiles_per_e
    n_tiles = tile_end[-1]
    row_start = jnp.cumsum(counts) - counts
    tile_ids = jnp.arange(t_max, dtype=jnp.int32)
    tile_e = jnp.searchsorted(tile_end, jnp.minimum(tile_ids, n_tiles - 1), side="right")
    tile_e = jnp.minimum(tile_e, N_EXPERTS - 1).astype(jnp.int32)
    row_in_e = (tile_ids - tile_start[tile_e])[:, None] * MOE_TILE + \
        jnp.arange(MOE_TILE, dtype=jnp.int32)[None, :]
    valid = (row_in_e < counts[tile_e][:, None]) & (tile_ids < n_tiles)[:, None]
    j = jnp.clip(row_start[tile_e][:, None] + row_in_e, 0, n_pairs - 1)
    pair = order[j]
    src_tok = jnp.where(valid, pair % n_tok, 0).astype(jnp.int32)
    spare = n_pairs + (tile_ids % 2)[:, None] * MOE_TILE + \
        jnp.arange(MOE_TILE, dtype=jnp.int32)[None, :]
    dst_row = jnp.where(valid, pair, spare).astype(jnp.int32)
    return (tile_e, n_tiles.reshape(1).astype(jnp.int32),
            src_tok.reshape(t_max, 1, MOE_TILE), dst_row.reshape(t_max, 1, MOE_TILE))


def kernel(x, w_in, attn_norm_w, conv_w, conv_b, dt_bias_fwd, a_log_fwd, dt_bias_bwd,
           a_log_bwd, d_skip, ssd_norm_w, w_out, ln1_w, ln1_b, router_w, router_b,
           w_gate, b_gate, w_up, b_up, w_down, b_down, ln2_w, ln2_b):
    b, s, d = x.shape
    n = b * s
    xf = x.reshape(n, d)
    cos, sin = _rope_tables(s)
    q_end = ATTN_WIDTH
    k_end = 2 * ATTN_WIDTH
    v_end = 3 * ATTN_WIDTH
    z_end = v_end + SSD_WIDTH
    xbc_end = z_end + SSD_CONV_CH
    in_cols = w_in.shape[-1]
    in_cols_pad = -(-in_cols // LANES) * LANES

    w_in_b = jnp.pad(w_in[0], ((0, 0), (0, in_cols_pad - in_cols))).astype(jnp.bfloat16)
    proj = _rows_matmul(xf, w_in_b, "in_proj").reshape(b, s, in_cols_pad)
    qh = proj[..., :q_end].reshape(b, s, ATTN_HEADS, HEAD_DIM)
    kh = proj[..., q_end:k_end].reshape(b, s, ATTN_HEADS, HEAD_DIM)
    vh = proj[..., k_end:v_end].reshape(b, s, ATTN_HEADS, HEAD_DIM)
    attn = _dilated_attention(_apply_rope(qh, cos, sin), _apply_rope(kh, cos, sin), vh)
    attn = _rms_norm(attn.reshape(b, s, ATTN_WIDTH), attn_norm_w[0])
    ssd = _bidirectional_ssd(proj[..., v_end:z_end], proj[..., z_end:xbc_end],
                             proj[..., xbc_end:in_cols],
                             conv_w[0], conv_b[0], dt_bias_fwd[0], a_log_fwd[0],
                             dt_bias_bwd[0], a_log_bwd[0], d_skip[0], ssd_norm_w[0])
    mix_in = jnp.concatenate([attn, ssd], axis=-1).reshape(n, d)
    mix = _rows_matmul(mix_in, w_out[0].astype(jnp.bfloat16), "out_proj")
    h1 = _layer_norm(DEEPNORM_ALPHA * xf + mix, ln1_w[0], ln1_b[0])

    logits = jnp.dot(h1, router_w[0], precision=lax.Precision.HIGHEST) + router_b[0]
    top_vals, top_idx = lax.top_k(logits, TOP_K)
    gates = jax.nn.softmax(top_vals, axis=-1)
    gates_pad = jnp.pad(gates, ((0, 0), (0, LANES - TOP_K)))

    tile_e, n_tiles, src_tok, dst_row = _routing_tables(top_idx, n)
    n_out_rows = TOP_K * n + 2 * MOE_TILE
    y_planes = _moe_ffn(h1, tile_e, n_tiles, src_tok, dst_row,
                        w_gate[0], b_gate[0], w_up[0], b_up[0], w_down[0], b_down[0],
                        n_out_rows)
    out = _combine_ln(y_planes, h1, gates_pad, ln2_w[0], ln2_b[0])
    return out.reshape(b, s, d)
```

```python
import functools
import math

import jax
import jax.numpy as jnp
from jax import lax
from jax.experimental import pallas as pl
from jax.experimental.pallas import tpu as pltpu

D_MODEL = 1024
HEAD_DIM = 64
ATTN_HEADS = 8
ATTN_WIDTH = ATTN_HEADS * HEAD_DIM
DILATED_PATTERNS = ((128, 1), (512, 4), (2048, 16))
ROPE_THETA = 10000.0
SSD_HEADS = 8
SSD_HEAD_DIM = 64
SSD_WIDTH = SSD_HEADS * SSD_HEAD_DIM
SSD_GROUPS = 2
SSD_STATE = 128
SSD_CONV = 5
SSD_CHUNK = 128
SSD_CONV_CH = SSD_WIDTH + 2 * SSD_GROUPS * SSD_STATE
N_EXPERTS = 32
TOP_K = 4
SWIGLU_ALPHA = 1.702
SWIGLU_LIMIT = 7.0
DEPTH = 1
DEEPNORM_ALPHA = (2.0 * DEPTH) ** 0.25
NORM_EPS = 1e-5

V7X_VMEM_BYTES = 64 * 1024 * 1024
LANES = 128

MATMUL_ROWS = 512
MOE_TILE = 512
FFN_CHUNK = 256
LN_ROWS = 512
SLAB = D_MODEL // LANES


def _rows_matmul_kernel(x_ref, w_ref, o_ref):
    o_ref[...] = jnp.dot(x_ref[...].astype(jnp.bfloat16), w_ref[...],
                         preferred_element_type=jnp.float32)


def _rows_matmul(x, w_bf16, name):
    n, k = x.shape
    _, m = w_bf16.shape
    assert n % MATMUL_ROWS == 0 and m % LANES == 0
    return pl.pallas_call(
        _rows_matmul_kernel,
        grid=(n // MATMUL_ROWS,),
        in_specs=[pl.BlockSpec((MATMUL_ROWS, k), lambda i: (i, 0)),
                  pl.BlockSpec((k, m), lambda i: (0, 0))],
        out_specs=pl.BlockSpec((MATMUL_ROWS, m), lambda i: (i, 0)),
        out_shape=jax.ShapeDtypeStruct((n, m), jnp.float32),
        compiler_params=pltpu.CompilerParams(
            dimension_semantics=("arbitrary",),
            vmem_limit_bytes=48 * 1024 * 1024),
        name=name,
    )(x, w_bf16)


ATTN_BLOCK = 128
ATTN_WINDOW = 256
ATTN_HALF = 64
ATTN_STAGE_ROWS = 256


def _strided_rows(start, size, stride):
    return pl.ds(start, size) if stride == 1 else pl.ds(start, size, stride=stride)


def _attention_kernel(q_ref, k_ref, v_ref, cos_ref, sin_lo_ref, sin_hi_ref, o_ref,
                      q0s, q1s, ks, vs, acc_o, acc_m, acc_l):
    s_len = q_ref.shape[0]
    head0 = lax.broadcasted_iota(jnp.int32, (1, LANES), 1) < HEAD_DIM
    q_i = lax.broadcasted_iota(jnp.int32, (ATTN_BLOCK, ATTN_WINDOW), 0)
    k_j = lax.broadcasted_iota(jnp.int32, (ATTN_BLOCK, ATTN_WINDOW), 1)
    rel = q_i - k_j
    last = len(DILATED_PATTERNS) - 1

    for pi, (window, dil) in enumerate(DILATED_PATTERNS):
        assert window // (2 * dil) == ATTN_HALF
        row_len = s_len // dil
        assert row_len % ATTN_WINDOW == 0 and row_len % ATTN_STAGE_ROWS == 0

        def stage(j, carry, dil=dil, row_len=row_len):
            f0 = pl.multiple_of(j * ATTN_STAGE_ROWS, ATTN_STAGE_ROWS)
            r = f0 // row_len
            src = _strided_rows((f0 - r * row_len) * dil + r, ATTN_STAGE_ROWS, dil)
            cos = cos_ref[src, :]
            sin_lo = sin_lo_ref[src, :]
            sin_hi = sin_hi_ref[src, :]

            def rope(t):
                return (t * cos + pltpu.roll(t, LANES - HEAD_DIM // 2, 1) * sin_lo
                        + pltpu.roll(t, HEAD_DIM // 2, 1) * sin_hi)

            qr = rope(q_ref[src, :]) * (HEAD_DIM ** -0.5)
            dst = pl.ds(f0, ATTN_STAGE_ROWS)
            q0s[dst, :] = jnp.where(head0, qr, 0.0).astype(jnp.bfloat16)
            q1s[dst, :] = jnp.where(head0, 0.0, qr).astype(jnp.bfloat16)
            ks[dst, :] = rope(k_ref[src, :]).astype(jnp.bfloat16)
            vs[dst, :] = v_ref[src, :].astype(jnp.bfloat16)
            return carry

        lax.fori_loop(0, s_len // ATTN_STAGE_ROWS, stage, 0, unroll=2)

        def block(bi, carry, pi=pi, dil=dil, row_len=row_len):
            f0 = pl.multiple_of(bi * ATTN_BLOCK, ATTN_BLOCK)
            r = f0 // row_len
            row_start = r * row_len
            win = jnp.clip(f0 - ATTN_HALF, row_start, row_start + row_len - ATTN_WINDOW)
            win = pl.multiple_of(win, ATTN_HALF)
            valid = jnp.abs(rel + (f0 - win)) <= ATTN_HALF
            kw = ks[pl.ds(win, ATTN_WINDOW), :]
            vw = vs[pl.ds(win, ATTN_WINDOW), :]
            parts = []
            for qs_ref in (q0s, q1s):
                qb = qs_ref[pl.ds(f0, ATTN_BLOCK), :]
                sc = lax.dot_general(qb, kw, (((1,), (1,)), ((), ())),
                                     preferred_element_type=jnp.float32)
                sc = jnp.where(valid, sc, -jnp.inf)
                m = jnp.max(sc, axis=-1, keepdims=True)
                p = jnp.exp(sc - m)
                den = jnp.sum(p, axis=-1, keepdims=True)
                num = jnp.dot(p.astype(jnp.bfloat16), vw, preferred_element_type=jnp.float32)
                parts.append((num, m, den))
            num = jnp.where(head0, parts[0][0], parts[1][0])
            m = jnp.where(head0, parts[0][1], parts[1][1])
            den = jnp.where(head0, parts[0][2], parts[1][2])
            rows = _strided_rows((f0 - row_start) * dil + r, ATTN_BLOCK, dil)
            if pi == 0:
                acc_o[rows, :] = num
                acc_m[rows, :] = m
                acc_l[rows, :] = den
            else:
                m_old = acc_m[rows, :]
                m_new = jnp.maximum(m_old, m)
                w_old = jnp.exp(m_old - m_new)
                w_cur = jnp.exp(m - m_new)
                num = acc_o[rows, :] * w_old + num * w_cur
                den = acc_l[rows, :] * w_old + den * w_cur
                if pi == last:
                    o_ref[rows, :] = num / den
                else:
                    acc_o[rows, :] = num
                    acc_m[rows, :] = m_new
                    acc_l[rows, :] = den
            return carry

        lax.fori_loop(0, s_len // ATTN_BLOCK, block, 0, unroll=4)


def _attention(proj):
    b, s, _ = proj.shape
    cos, sin = _rope_tables(s)
    reps = LANES // (HEAD_DIM // 2)
    first_half = (jnp.arange(LANES) % HEAD_DIM) < HEAD_DIM // 2
    cos_t = jnp.tile(cos, (1, reps))
    sin_t = jnp.tile(sin, (1, reps))
    sin_lo = jnp.where(first_half, -sin_t, 0.0)
    sin_hi = jnp.where(first_half, 0.0, sin_t)
    pairs = ATTN_WIDTH // LANES

    def cols(off):
        return pl.BlockSpec((None, s, LANES), lambda bi, hp, off=off: (bi, 0, off + hp))

    table = pl.BlockSpec((s, LANES), lambda bi, hp: (0, 0))
    return pl.pallas_call(
        _attention_kernel,
        grid=(b, pairs),
        in_specs=[cols(0), cols(pairs), cols(2 * pairs), table, table, table],
        out_specs=pl.BlockSpec((None, s, LANES), lambda bi, hp: (bi, 0, hp)),
        out_shape=jax.ShapeDtypeStruct((b, s, ATTN_WIDTH), jnp.float32),
        scratch_shapes=[pltpu.VMEM((s, LANES), jnp.bfloat16)] * 4
        + [pltpu.VMEM((s, LANES), jnp.float32)] * 3,
        compiler_params=pltpu.CompilerParams(
            dimension_semantics=("arbitrary", "arbitrary"),
            vmem_limit_bytes=48 * 1024 * 1024),
        name="dilated_attention",
    )(proj, proj, proj, cos_t, sin_lo, sin_hi)


SSD_HALO = 8
SSD_GROUP_HEADS = SSD_HEADS // SSD_GROUPS
SSD_GROUP_WIDTH = SSD_GROUP_HEADS * SSD_HEAD_DIM


def _softplus(v):
    return jnp.maximum(v, 0.0) + jnp.log(1.0 + jnp.exp(-jnp.abs(v)))


def _expand_heads(mat, first_head):
    t = mat.shape[0]
    low = lax.broadcasted_iota(jnp.int32, (1, LANES), 1) < SSD_HEAD_DIM
    tiles = []
    for j in range(0, SSD_GROUP_HEADS, 2):
        a = jnp.broadcast_to(mat[:, first_head + j:first_head + j + 1], (t, LANES))
        b = jnp.broadcast_to(mat[:, first_head + j + 1:first_head + j + 2], (t, LANES))
        tiles.append(jnp.where(low, a, b))
    return jnp.concatenate(tiles, axis=-1)


def _ssd_kernel(xbc_ref, prev_ref, next_ref, dt_ref, cw_ref, cb_ref, dtb_ref, alog_ref, dsk_ref,
                y_ref, ext, cv, state, *, reverse):
    i = pl.program_id(1)
    nc = pl.num_programs(1)
    c = (nc - 1 - i) if reverse else i
    t = SSD_CHUNK
    bf16 = jnp.bfloat16

    @pl.when(i == 0)
    def _():
        state[...] = jnp.zeros_like(state)

    ext[0:SSD_HALO, :] = jnp.where(c > 0, prev_ref[...], 0.0)
    ext[SSD_HALO:SSD_HALO + t, :] = xbc_ref[...]
    ext[SSD_HALO + t:, :] = jnp.where(c < nc - 1, next_ref[...], 0.0)
    for j in range(SSD_CONV_CH // LANES):
        cols = slice(j * LANES, (j + 1) * LANES)
        acc = jnp.broadcast_to(cb_ref[:, cols], (t, LANES))
        for k in range(SSD_CONV):
            acc = acc + cw_ref[k:k + 1, cols] * ext[pl.ds(SSD_HALO - SSD_CONV // 2 + k, t), cols]
        cv[:, cols] = acc * jax.nn.sigmoid(acc)

    dt = _softplus(dt_ref[...] + dtb_ref[...])
    a = dt * (-jnp.exp(alog_ref[...]))
    row = lax.broadcasted_iota(jnp.int32, (t, LANES), 0)
    lane = lax.broadcasted_iota(jnp.int32, (t, LANES), 1)
    cs = a
    shift = 1
    while shift < t:
        cs = cs + jnp.where(row >= shift, pltpu.roll(cs, shift, 0), 0.0)
        shift *= 2
    total = cs[t - 1:t, :]
    e = jnp.where(lane < SSD_HEADS, cs, total - cs + a)
    off = SSD_HEADS if reverse else 0
    edge = e[0:1, :] if reverse else e[t - 1:t, :]
    scale_y = jnp.exp(e)
    scale_s = jnp.exp(edge - e) * dt

    x_off = 0
    b_off = SSD_WIDTH
    c_off = SSD_WIDTH + SSD_GROUPS * SSD_STATE
    low = lax.broadcasted_iota(jnp.int32, (1, LANES), 1) < SSD_HEAD_DIM

    if not reverse:
        e_t = e.T
        dt_t = dt.T
        li = lax.broadcasted_iota(jnp.int32, (t, t), 0)
        si = lax.broadcasted_iota(jnp.int32, (t, t), 1)
        lower = si <= li
        upper = si >= li

    for g in range(SSD_GROUPS):
        bg = cv[:, b_off + g * SSD_STATE:b_off + (g + 1) * SSD_STATE].astype(bf16)
        cg = cv[:, c_off + g * SSD_STATE:c_off + (g + 1) * SSD_STATE].astype(bf16)
        xg = cv[:, x_off + g * SSD_GROUP_WIDTH:x_off + (g + 1) * SSD_GROUP_WIDTH]
        h0 = g * SSD_GROUP_HEADS
        st = state[g]
        y = jnp.dot(cg, st.astype(bf16), preferred_element_type=jnp.float32) \
            * _expand_heads(scale_y, off + h0)
        xw = (xg * _expand_heads(scale_s, off + h0)).astype(bf16)
        new = lax.dot_general(bg, xw, (((0,), (0,)), ((), ())),
                              preferred_element_type=jnp.float32)
        decay = _expand_heads(jnp.broadcast_to(jnp.exp(edge), (SSD_HALO, LANES)), off + h0)
        state[g] = st * decay[0:1, :] + new

        if not reverse:
            cbm = lax.dot_general(cg, bg, (((1,), (1,)), ((), ())),
                                  preferred_element_type=jnp.float32)
            pieces = []
            for j in range(0, SSD_GROUP_HEADS, 2):
                xp = xg[:, j * SSD_HEAD_DIM:(j + 2) * SSD_HEAD_DIM]
                yp = None
                for hh, keep in ((j, low), (j + 1, jnp.logical_not(low))):
                    h = h0 + hh
                    hb = SSD_HEADS + h
                    lf = jnp.exp(jnp.where(lower, e[:, h:h + 1] - e_t[h:h + 1, :], -jnp.inf))
                    lb = jnp.exp(jnp.where(upper, e[:, hb:hb + 1] - e_t[hb:hb + 1, :], -jnp.inf))
                    mh = cbm * (lf * dt_t[h:h + 1, :] + lb * dt_t[hb:hb + 1, :])
                    part = jnp.dot(mh.astype(bf16), jnp.where(keep, xp, 0.0).astype(bf16),
                                   preferred_element_type=jnp.float32)
                    yp = part if yp is None else yp + part
                pieces.append(yp)
            gcols = slice(g * SSD_GROUP_WIDTH, (g + 1) * SSD_GROUP_WIDTH)
            y = y + jnp.concatenate(pieces, axis=-1) + dsk_ref[:, gcols] * xg
        y_ref[:, g * SSD_GROUP_WIDTH:(g + 1) * SSD_GROUP_WIDTH] = y


def _ssd_scan(proj, conv_w, conv_b, dt_bias, a_log, d_skip, reverse):
    b, s, _ = proj.shape
    nc = s // SSD_CHUNK
    z_end = 3 * ATTN_WIDTH + SSD_WIDTH
    xbc_blk = z_end // SSD_CONV_CH
    dt_blk = (z_end + SSD_CONV_CH) // LANES
    assert z_end % SSD_CONV_CH == 0 and (z_end + SSD_CONV_CH) % LANES == 0
    halo_per_chunk = SSD_CHUNK // SSD_HALO
    n_halo = s // SSD_HALO

    def chunk(i):
        return (nc - 1 - i) if reverse else i

    cw = jnp.pad(conv_w, ((0, SSD_HALO - SSD_CONV), (0, 0)))
    pad16 = lambda v: jnp.pad(v, (0, LANES - 2 * SSD_HEADS)).reshape(1, LANES)
    vec = lambda width: pl.BlockSpec((1, width), lambda bi, i: (0, 0))
    return pl.pallas_call(
        functools.partial(_ssd_kernel, reverse=reverse),
        grid=(b, nc),
        in_specs=[
            pl.BlockSpec((None, SSD_CHUNK, SSD_CONV_CH), lambda bi, i: (bi, chunk(i), xbc_blk)),
            pl.BlockSpec((None, SSD_HALO, SSD_CONV_CH),
                         lambda bi, i: (bi, jnp.maximum(chunk(i) * halo_per_chunk - 1, 0), xbc_blk)),
            pl.BlockSpec((None, SSD_HALO, SSD_CONV_CH),
                         lambda bi, i: (bi, jnp.minimum((chunk(i) + 1) * halo_per_chunk, n_halo - 1),
                                        xbc_blk)),
            pl.BlockSpec((None, SSD_CHUNK, LANES), lambda bi, i: (bi, chunk(i), dt_blk)),
            pl.BlockSpec((SSD_HALO, SSD_CONV_CH), lambda bi, i: (0, 0)),
            vec(SSD_CONV_CH), vec(LANES), vec(LANES), vec(SSD_WIDTH),
        ],
        out_specs=pl.BlockSpec((None, SSD_CHUNK, SSD_WIDTH), lambda bi, i: (bi, chunk(i), 0)),
        out_shape=jax.ShapeDtypeStruct((b, s, SSD_WIDTH), jnp.float32),
        scratch_shapes=[
            pltpu.VMEM((SSD_CHUNK + 2 * SSD_HALO, SSD_CONV_CH), jnp.float32),
            pltpu.VMEM((SSD_CHUNK, SSD_CONV_CH), jnp.float32),
            pltpu.VMEM((SSD_GROUPS, SSD_STATE, SSD_GROUP_WIDTH), jnp.float32),
        ],
        compiler_params=pltpu.CompilerParams(
            dimension_semantics=("arbitrary", "arbitrary"),
            vmem_limit_bytes=32 * 1024 * 1024),
        name="ssd_bwd" if reverse else "ssd_fwd",
    )(proj, proj, proj, proj, cw, conv_b.reshape(1, SSD_CONV_CH), pad16(dt_bias), pad16(a_log),
      jnp.repeat(d_skip, SSD_HEAD_DIM).reshape(1, SSD_WIDTH))


def _ssd_pre_gate(proj, conv_w, conv_b, dt_bias_fwd, a_log_fwd, dt_bias_bwd, a_log_bwd, d_skip):
    dt_bias = jnp.concatenate([dt_bias_fwd, dt_bias_bwd])
    a_log = jnp.concatenate([a_log_fwd, a_log_bwd])
    y_f = _ssd_scan(proj, conv_w, conv_b, dt_bias, a_log, d_skip, reverse=False)
    y_b = _ssd_scan(proj, conv_w, conv_b, dt_bias, a_log, d_skip, reverse=True)
    return y_f, y_b


def _moe_kernel(te_ref, nt_ref,
                src_cur_ref, src_nxt_ref, dst_ref,
                h_hbm, wg_ref, wu_ref, wd_ref, bg_ref, bu_ref, bd_ref,
                y_hbm,
                xbuf, ybuf, wg_bf, wu_bf, wd_bf, gsem, ssem):
    i = pl.program_id(0)
    n_tiles = nt_ref[0]
    slot = i % 2

    def slab(row):
        return pl.ds(pl.multiple_of(row * SLAB, SLAB), SLAB)

    def gather_copy(tok, row, s):
        return pltpu.make_async_copy(h_hbm.at[slab(tok), :],
                                     xbuf.at[s, slab(row), :], gsem.at[s])

    def scatter_copy(dst, row, s):
        return pltpu.make_async_copy(ybuf.at[s, slab(row), :],
                                     y_hbm.at[slab(dst), :], ssem.at[s])

    def start_gather(idx_ref, s):
        def body(r, c):
            gather_copy(idx_ref[0, 0, r], r, s).start()
            return c
        lax.fori_loop(0, MOE_TILE, body, 0, unroll=8)

    def wait_gather(s):
        def body(r, c):
            gather_copy(0, r, s).wait()
            return c
        lax.fori_loop(0, MOE_TILE, body, 0, unroll=8)

    def wait_scatter(s):
        def body(r, c):
            scatter_copy(0, r, s).wait()
            return c
        lax.fori_loop(0, MOE_TILE, body, 0, unroll=8)

    @pl.when(i == 0)
    def _init_spare():
        ybuf[0] = jnp.zeros((MOE_TILE * SLAB, LANES), jnp.float32)
        n_pairs = y_hbm.shape[0] // SLAB - 2 * MOE_TILE
        for half in range(2):
            cp = pltpu.make_async_copy(
                ybuf.at[0],
                y_hbm.at[pl.ds((n_pairs + half * MOE_TILE) * SLAB, MOE_TILE * SLAB), :],
                ssem.at[half])
            cp.start()
            cp.wait()

    @pl.when(i < n_tiles)
    def _active():
        @pl.when(i == 0)
        def _():
            start_gather(src_cur_ref, 0)

        wait_gather(slot)

        @pl.when(i + 1 < n_tiles)
        def _():
            start_gather(src_nxt_ref, 1 - slot)

        e = te_ref[i]
        prev_e = te_ref[jnp.maximum(i - 1, 0)]

        @pl.when((i == 0) | (e != prev_e))
        def _():
            wg_bf[...] = wg_ref[0].astype(jnp.bfloat16)
            wu_bf[...] = wu_ref[0].astype(jnp.bfloat16)
            wd_bf[...] = wd_ref[0].astype(jnp.bfloat16)

        xb = jnp.concatenate(
            [xbuf[slot, pl.ds(c, MOE_TILE, stride=SLAB), :].astype(jnp.bfloat16)
             for c in range(SLAB)], axis=-1)
        acc = jnp.zeros((MOE_TILE, D_MODEL), jnp.float32)
        for c in range(D_MODEL // FFN_CHUNK):
            cols = slice(c * FFN_CHUNK, (c + 1) * FFN_CHUNK)
            g = jnp.dot(xb, wg_bf[:, cols], preferred_element_type=jnp.float32)
            g = g + bg_ref[pl.ds(e, 1), cols]
            u = jnp.dot(xb, wu_bf[:, cols], preferred_element_type=jnp.float32)
            u = u + bu_ref[pl.ds(e, 1), cols]
            g = jnp.minimum(g, SWIGLU_LIMIT)
            u = jnp.clip(u, -SWIGLU_LIMIT, SWIGLU_LIMIT)
            act = (u + 1.0) * g * jax.nn.sigmoid(SWIGLU_ALPHA * g)
            acc = acc + jnp.dot(act.astype(jnp.bfloat16), wd_bf[cols, :],
                                preferred_element_type=jnp.float32)
        y = acc + bd_ref[pl.ds(e, 1), :]

        @pl.when(i >= 2)
        def _():
            wait_scatter(slot)

        for c in range(SLAB):
            ybuf[slot, pl.ds(c, MOE_TILE, stride=SLAB), :] = y[:, c * LANES:(c + 1) * LANES]

        def sbody(r, c):
            scatter_copy(dst_ref[0, 0, r], r, slot).start()
            return c
        lax.fori_loop(0, MOE_TILE, sbody, 0, unroll=8)

        @pl.when(i == n_tiles - 1)
        def _():
            @pl.when(i >= 1)
            def _():
                wait_scatter(1 - slot)
            wait_scatter(slot)


def _moe_ffn(h1, tile_e, n_tiles, src_tok, dst_row, w_gate, b_gate, w_up, b_up,
             w_down, b_down, n_out_rows):
    t_max = tile_e.shape[0]
    d = D_MODEL
    idx_cur = pl.BlockSpec((1, 1, MOE_TILE), lambda i, te, nt: (i, 0, 0),
                           memory_space=pltpu.SMEM)
    idx_nxt = pl.BlockSpec((1, 1, MOE_TILE),
                           lambda i, te, nt: (jnp.minimum(i + 1, t_max - 1), 0, 0),
                           memory_space=pltpu.SMEM)
    w_spec = pl.BlockSpec((1, d, d), lambda i, te, nt: (te[i], 0, 0))
    b_spec = pl.BlockSpec((N_EXPERTS, d), lambda i, te, nt: (0, 0))
    grid_spec = pltpu.PrefetchScalarGridSpec(
        num_scalar_prefetch=2,
        grid=(t_max,),
        in_specs=[idx_cur, idx_nxt, idx_cur,
                  pl.BlockSpec(memory_space=pl.ANY),
                  w_spec, w_spec, w_spec, b_spec, b_spec, b_spec],
        out_specs=pl.BlockSpec(memory_space=pl.ANY),
        scratch_shapes=[
            pltpu.VMEM((2, MOE_TILE * SLAB, LANES), jnp.float32),
            pltpu.VMEM((2, MOE_TILE * SLAB, LANES), jnp.float32),
            pltpu.VMEM((d, d), jnp.bfloat16),
            pltpu.VMEM((d, d), jnp.bfloat16),
            pltpu.VMEM((d, d), jnp.bfloat16),
            pltpu.SemaphoreType.DMA((2,)),
            pltpu.SemaphoreType.DMA((2,)),
        ],
    )
    return pl.pallas_call(
        _moe_kernel,
        grid_spec=grid_spec,
        out_shape=jax.ShapeDtypeStruct((n_out_rows * SLAB, LANES), jnp.float32),
        compiler_params=pltpu.CompilerParams(
            dimension_semantics=("arbitrary",),
            vmem_limit_bytes=V7X_VMEM_BYTES - 6 * 1024 * 1024),
        name="moe_ffn",
    )(tile_e, n_tiles, src_tok, src_tok, dst_row, h1,
      w_gate, w_up, w_down, b_gate, b_up, b_down)


def _combine_ln_kernel(y0_ref, y1_ref, y2_ref, y3_ref, h_ref, g_ref, w_ref, b_ref, o_ref):
    def rows(ref):
        return jnp.concatenate(
            [ref[pl.ds(c, LN_ROWS, stride=SLAB), :] for c in range(SLAB)], axis=-1)

    g = g_ref[...]
    ffn = (g[:, 0:1] * rows(y0_ref) + g[:, 1:2] * rows(y1_ref)
           + g[:, 2:3] * rows(y2_ref) + g[:, 3:4] * rows(y3_ref))
    v = DEEPNORM_ALPHA * h_ref[...] + ffn
    mu = jnp.mean(v, axis=-1, keepdims=True)
    var = jnp.mean(jnp.square(v - mu), axis=-1, keepdims=True)
    o_ref[...] = (v - mu) * lax.rsqrt(var + NORM_EPS) * w_ref[...] + b_ref[...]


def _combine_ln(y_planes, h1, gates_pad, ln_w, ln_b):
    n, d = h1.shape
    nb = n // LN_ROWS

    def plane(k):
        return pl.BlockSpec((LN_ROWS * SLAB, LANES), lambda i, k=k: (k * nb + i, 0))

    row = pl.BlockSpec((LN_ROWS, d), lambda i: (i, 0))
    vec = pl.BlockSpec((1, d), lambda i: (0, 0))
    return pl.pallas_call(
        _combine_ln_kernel,
        grid=(nb,),
        in_specs=[plane(0), plane(1), plane(2), plane(3), row,
                  pl.BlockSpec((LN_ROWS, LANES), lambda i: (i, 0)), vec, vec],
        out_specs=row,
        out_shape=jax.ShapeDtypeStruct((n, d), jnp.float32),
        compiler_params=pltpu.CompilerParams(
            dimension_semantics=("arbitrary",),
            vmem_limit_bytes=48 * 1024 * 1024),
        name="combine_ln",
    )(y_planes, y_planes, y_planes, y_planes, h1, gates_pad,
      ln_w.reshape(1, d), ln_b.reshape(1, d))


def _layer_norm(x, w, b):
    mu = jnp.mean(x, axis=-1, keepdims=True)
    var = jnp.mean(jnp.square(x - mu), axis=-1, keepdims=True)
    return (x - mu) * lax.rsqrt(var + NORM_EPS) * w + b


def _rms_norm(x, w):
    return x * lax.rsqrt(jnp.mean(jnp.square(x), axis=-1, keepdims=True) + NORM_EPS) * w


def _rope_tables(s):
    pos = jnp.arange(s, dtype=jnp.float32)
    inv_freq = ROPE_THETA ** (-jnp.arange(0, HEAD_DIM, 2, dtype=jnp.float32) / HEAD_DIM)
    ang = pos[:, None] * inv_freq[None, :]
    return jnp.cos(ang), jnp.sin(ang)


def _apply_rope(t, cos, sin):
    t1, t2 = jnp.split(t, 2, axis=-1)
    c = cos[None, :, None, :]
    s = sin[None, :, None, :]
    return jnp.concatenate([t1 * c - t2 * s, t2 * c + t1 * s], axis=-1)


def _banded_window_attention(q, k, v, half):
    b, g, L, h, dh = q.shape
    blk = half
    nb = -(-L // blk)
    lp = nb * blk
    q = jnp.pad(q, ((0, 0), (0, 0), (0, lp - L), (0, 0), (0, 0)))
    kv_pad = ((0, 0), (0, 0), (blk, lp - L + blk), (0, 0), (0, 0))
    kb = jnp.pad(k, kv_pad).reshape(b, g, nb + 2, blk, h, dh)
    vb = jnp.pad(v, kv_pad).reshape(b, g, nb + 2, blk, h, dh)
    k_win = jnp.concatenate([kb[:, :, 0:nb], kb[:, :, 1:nb + 1], kb[:, :, 2:nb + 2]], axis=3)
    v_win = jnp.concatenate([vb[:, :, 0:nb], vb[:, :, 1:nb + 1], vb[:, :, 2:nb + 2]], axis=3)
    qb = q.reshape(b, g, nb, blk, h, dh)
    scores = jnp.einsum("bgnqhd,bgnkhd->bgnhqk", qb, k_win) * (dh ** -0.5)
    blocks = jnp.arange(nb)[:, None, None]
    qpos = blocks * blk + jnp.arange(blk)[None, :, None]
    kpos = (blocks - 1) * blk + jnp.arange(3 * blk)[None, None, :]
    valid = (jnp.abs(qpos - kpos) <= half) & (kpos >= 0) & (kpos < L)
    scores = jnp.where(valid[None, None, :, None], scores, -jnp.inf)
    mx = jnp.max(scores, axis=-1)
    p = jnp.exp(scores - mx[..., None])
    den = jnp.sum(p, axis=-1)
    num = jnp.einsum("bgnhqk,bgnkhd->bgnqhd", p, v_win).reshape(b, g, lp, h, dh)[:, :, :L]
    den = den.transpose(0, 1, 2, 4, 3).reshape(b, g, lp, h)[:, :, :L]
    mx = mx.transpose(0, 1, 2, 4, 3).reshape(b, g, lp, h)[:, :, :L]
    return num, den, mx


def _dilated_attention(q, k, v):
    b, s, h, dh = q.shape
    nums, dens, mxs = [], [], []
    for window, dil in DILATED_PATTERNS:
        half = window // (2 * dil)
        L = s // dil
        qr = q.reshape(b, L, dil, h, dh).transpose(0, 2, 1, 3, 4)
        kr = k.reshape(b, L, dil, h, dh).transpose(0, 2, 1, 3, 4)
        vr = v.reshape(b, L, dil, h, dh).transpose(0, 2, 1, 3, 4)
        num, den, mx = _banded_window_attention(qr, kr, vr, half)
        nums.append(num.transpose(0, 2, 1, 3, 4).reshape(b, s, h, dh))
        dens.append(den.transpose(0, 2, 1, 3).reshape(b, s, h))
        mxs.append(mx.transpose(0, 2, 1, 3).reshape(b, s, h))
    num_all = jnp.stack(nums)
    den_all = jnp.stack(dens)
    mx_all = jnp.stack(mxs)
    w = jnp.exp(mx_all - jnp.max(mx_all, axis=0, keepdims=True))
    return jnp.sum(w[..., None] * num_all, axis=0) / jnp.sum(w * den_all, axis=0)[..., None]


def _ssd_chunked(xs, a, bm, cm):
    b, l, h, p = xs.shape
    n = bm.shape[-1]
    t = SSD_CHUNK
    c = l // t
    xs = xs.reshape(b, c, t, h, p)
    bm = bm.reshape(b, c, t, h, n)
    cm = cm.reshape(b, c, t, h, n)
    a = a.reshape(b, c, t, h).transpose(0, 3, 1, 2)
    a_cs = jnp.cumsum(a, axis=-1)
    seg = a_cs[..., :, None] - a_cs[..., None, :]
    tril = jnp.tril(jnp.ones((t, t), dtype=bool))
    lmat = jnp.exp(jnp.where(tril, seg, -jnp.inf))
    cb = jnp.einsum("bclhn,bcshn->bhcls", cm, bm)
    y_diag = jnp.einsum("bhcls,bcshp->bclhp", cb * lmat, xs)
    decay_states = jnp.exp(a_cs[..., -1:] - a_cs)
    states = jnp.einsum("bclhn,bhcl,bclhp->bchpn", bm, decay_states, xs)
    chunk_decay = jnp.exp(a_cs[..., -1])

    def step(state, inp):
        st, dec = inp
        return state * dec[..., None, None] + st, state

    init = jnp.zeros((b, h, p, n), dtype=jnp.float32)
    _, prev = lax.scan(step, init, (states.transpose(1, 0, 2, 3, 4), chunk_decay.transpose(2, 0, 1)))
    prev = prev.transpose(1, 0, 2, 3, 4)
    y_off = jnp.einsum("bclhn,bchpn,bhcl->bclhp", cm, prev, jnp.exp(a_cs))
    return (y_diag + y_off).reshape(b, l, h, p)


def _bidirectional_ssd(z, xbc, dt_raw, conv_w, conv_b, dt_bias_fwd, a_log_fwd,
                       dt_bias_bwd, a_log_bwd, d_skip, norm_w):
    b, s, _ = z.shape
    xbc = lax.conv_general_dilated(
        xbc, conv_w[:, None, :], (1,),
        [(SSD_CONV // 2, SSD_CONV // 2)], dimension_numbers=("NWC", "WIO", "NWC"),
        feature_group_count=SSD_CONV_CH) + conv_b
    xbc = jax.nn.silu(xbc)
    gn = SSD_GROUPS * SSD_STATE
    xs = xbc[..., :SSD_WIDTH].reshape(b, s, SSD_HEADS, SSD_HEAD_DIM)
    bm = xbc[..., SSD_WIDTH:SSD_WIDTH + gn].reshape(b, s, SSD_GROUPS, SSD_STATE)
    cm = xbc[..., SSD_WIDTH + gn:].reshape(b, s, SSD_GROUPS, SSD_STATE)
    heads_per_group = SSD_HEADS // SSD_GROUPS
    bm = jnp.repeat(bm, heads_per_group, axis=2)
    cm = jnp.repeat(cm, heads_per_group, axis=2)
    dt_f = jax.nn.softplus(dt_raw[..., :SSD_HEADS] + dt_bias_fwd)
    dt_b = jax.nn.softplus(dt_raw[..., SSD_HEADS:] + dt_bias_bwd)
    a_f = -jnp.exp(a_log_fwd)
    a_b = -jnp.exp(a_log_bwd)
    y_f = _ssd_chunked(xs * dt_f[..., None], dt_f * a_f, bm, cm)
    flip = lambda t: jnp.flip(t, axis=1)
    y_b = flip(_ssd_chunked(flip(xs * dt_b[..., None]), flip(dt_b * a_b), flip(bm), flip(cm)))
    y = y_f + y_b + d_skip[:, None] * xs
    y = y.reshape(b, s, SSD_WIDTH) * jax.nn.silu(z)
    yg = y.reshape(b, s, SSD_GROUPS, SSD_WIDTH // SSD_GROUPS)
    yg = yg * lax.rsqrt(jnp.mean(jnp.square(yg), axis=-1, keepdims=True) + NORM_EPS)
    return yg.reshape(b, s, SSD_WIDTH) * norm_w


def _routing_tables(top_idx, n_tok):
    n_pairs = TOP_K * n_tok
    t_max = n_pairs // MOE_TILE + N_EXPERTS
    flat_e = top_idx.T.reshape(-1).astype(jnp.int32)
    order = jnp.argsort(flat_e, stable=True).astype(jnp.int32)
    counts = jnp.sum((flat_e[:, None] == jnp.arange(N_EXPERTS, dtype=jnp.int32)[None, :])
                     .astype(jnp.int32), axis=0)
    tiles_per_e = (counts + MOE_TILE - 1) // MOE_TILE
    tile_end = jnp.cumsum(tiles_per_e)
    tile_start = tile_end - tiles_per_e
    n_tiles = tile_end[-1]
    row_start = jnp.cumsum(counts) - counts
    tile_ids = jnp.arange(t_max, dtype=jnp.int32)
    tile_e = jnp.searchsorted(tile_end, jnp.minimum(tile_ids, n_tiles - 1), side="right")
    tile_e = jnp.minimum(tile_e, N_EXPERTS - 1).astype(jnp.int32)
    row_in_e = (tile_ids - tile_start[tile_e])[:, None] * MOE_TILE + \
        jnp.arange(MOE_TILE, dtype=jnp.int32)[None, :]
    valid = (row_in_e < counts[tile_e][:, None]) & (tile_ids < n_tiles)[:, None]
    j = jnp.clip(row_start[tile_e][:, None] + row_in_e, 0, n_pairs - 1)
    pair = order[j]
    src_tok = jnp.where(valid, pair % n_tok, 0).astype(jnp.int32)
    spare = n_pairs + (tile_ids % 2)[:, None] * MOE_TILE + \
        jnp.arange(MOE_TILE, dtype=jnp.int32)[None, :]
    dst_row = jnp.where(valid, pair, spare).astype(jnp.int32)
    return (tile_e, n_tiles.reshape(1).astype(jnp.int32),
            src_tok.reshape(t_max, 1, MOE_TILE), dst_row.reshape(t_max, 1, MOE_TILE))


def kernel(x, w_in, attn_norm_w, conv_w, conv_b, dt_bias_fwd, a_log_fwd, dt_bias_bwd,
           a_log_bwd, d_skip, ssd_norm_w, w_out, ln1_w, ln1_b, router_w, router_b,
           w_gate, b_gate, w_up, b_up, w_down, b_down, ln2_w, ln2_b):
    b, s, d = x.shape
    n = b * s
    xf = x.reshape(n, d)
    cos, sin = _rope_tables(s)
    q_end = ATTN_WIDTH
    k_end = 2 * ATTN_WIDTH
    v_end = 3 * ATTN_WIDTH
    z_end = v_end + SSD_WIDTH
    xbc_end = z_end + SSD_CONV_CH
    in_cols = w_in.shape[-1]
    in_cols_pad = -(-in_cols // LANES) * LANES

    w_in_b = jnp.pad(w_in[0], ((0, 0), (0, in_cols_pad - in_cols))).astype(jnp.bfloat16)
    proj = _rows_matmul(xf, w_in_b, "in_proj").reshape(b, s, in_cols_pad)
    attn = _rms_norm(_attention(proj), attn_norm_w[0])
    y_f, y_b = _ssd_pre_gate(proj, conv_w[0], conv_b[0], dt_bias_fwd[0], a_log_fwd[0],
                             dt_bias_bwd[0], a_log_bwd[0], d_skip[0])
    yg = ((y_f + y_b) * jax.nn.silu(proj[..., v_end:z_end])).reshape(
        b, s, SSD_GROUPS, SSD_GROUP_WIDTH)
    yg = yg * lax.rsqrt(jnp.mean(jnp.square(yg), axis=-1, keepdims=True) + NORM_EPS)
    ssd = yg.reshape(b, s, SSD_WIDTH) * ssd_norm_w[0]
    mix_in = jnp.concatenate([attn, ssd], axis=-1).reshape(n, d)
    mix = _rows_matmul(mix_in, w_out[0].astype(jnp.bfloat16), "out_proj")
    h1 = _layer_norm(DEEPNORM_ALPHA * xf + mix, ln1_w[0], ln1_b[0])

    logits = jnp.dot(h1, router_w[0], precision=lax.Precision.HIGHEST) + router_b[0]
    top_vals, top_idx = lax.top_k(logits, TOP_K)
    gates = jax.nn.softmax(top_vals, axis=-1)
    gates_pad = jnp.pad(gates, ((0, 0), (0, LANES - TOP_K)))

    tile_e, n_tiles, src_tok, dst_row = _routing_tables(top_idx, n)
    n_out_rows = TOP_K * n + 2 * MOE_TILE
    y_planes = _moe_ffn(h1.reshape(n * SLAB, LANES), tile_e, n_tiles, src_tok, dst_row,
                        w_gate[0], b_gate[0], w_up[0], b_up[0], w_down[0], b_down[0],
                        n_out_rows)
    out = _combine_ln(y_planes, h1, gates_pad, ln2_w[0], ln2_b[0])
    return out.reshape(b, s, d)
```

```python
import functools

import jax
import jax.numpy as jnp
from jax import lax
from jax.experimental import pallas as pl
from jax.experimental.pallas import tpu as pltpu

D_MODEL = 1024
HEAD_DIM = 64
ATTN_HEADS = 8
ATTN_WIDTH = ATTN_HEADS * HEAD_DIM
DILATED_PATTERNS = ((128, 1), (512, 4), (2048, 16))
ROPE_THETA = 10000.0
SSD_HEADS = 8
SSD_HEAD_DIM = 64
SSD_WIDTH = SSD_HEADS * SSD_HEAD_DIM
SSD_GROUPS = 2
SSD_STATE = 128
SSD_CONV = 5
SSD_CHUNK = 128
SSD_CONV_CH = SSD_WIDTH + 2 * SSD_GROUPS * SSD_STATE
N_EXPERTS = 32
TOP_K = 4
SWIGLU_ALPHA = 1.702
SWIGLU_LIMIT = 7.0
DEPTH = 1
DEEPNORM_ALPHA = (2.0 * DEPTH) ** 0.25
NORM_EPS = 1e-5

V7X_VMEM_BYTES = 64 * 1024 * 1024
LANES = 128

MATMUL_ROWS = 512
MIX_ROWS = 512
MOE_TILE = 512
FFN_CHUNK = 256
LN_ROWS = 512
SLAB = D_MODEL // LANES


def _slab_rows(ref, rows):
    return jnp.concatenate(
        [ref[pl.ds(c, rows, stride=SLAB), :] for c in range(SLAB)], axis=-1)


def _rows_matmul_kernel(x_ref, w_ref, o_ref):
    o_ref[...] = jnp.dot(x_ref[...].astype(jnp.bfloat16), w_ref[...],
                         preferred_element_type=jnp.float32)


def _rows_matmul(x, w_bf16, name):
    n, k = x.shape
    _, m = w_bf16.shape
    assert n % MATMUL_ROWS == 0 and m % LANES == 0
    return pl.pallas_call(
        _rows_matmul_kernel,
        grid=(n // MATMUL_ROWS,),
        in_specs=[pl.BlockSpec((MATMUL_ROWS, k), lambda i: (i, 0)),
                  pl.BlockSpec((k, m), lambda i: (0, 0))],
        out_specs=pl.BlockSpec((MATMUL_ROWS, m), lambda i: (i, 0)),
        out_shape=jax.ShapeDtypeStruct((n, m), jnp.float32),
        compiler_params=pltpu.CompilerParams(
            dimension_semantics=("arbitrary",),
            vmem_limit_bytes=48 * 1024 * 1024),
        name=name,
    )(x, w_bf16)


ATTN_BLOCK = 128
ATTN_WINDOW = 256
ATTN_HALF = 64
ATTN_STAGE_ROWS = 256


def _rope_tables(s):
    pos = jnp.arange(s, dtype=jnp.float32)
    inv_freq = ROPE_THETA ** (-jnp.arange(0, HEAD_DIM, 2, dtype=jnp.float32) / HEAD_DIM)
    ang = pos[:, None] * inv_freq[None, :]
    return jnp.cos(ang), jnp.sin(ang)


def _strided_rows(start, size, stride):
    return pl.ds(start, size) if stride == 1 else pl.ds(start, size, stride=stride)


def _attention_kernel(q_ref, k_ref, v_ref, cos_ref, sin_lo_ref, sin_hi_ref, o_ref,
                      qr, kr, q0s, q1s, ks, vs, acc_o, acc_m, acc_l):
    s_len = q_ref.shape[0]
    head0 = lax.broadcasted_iota(jnp.int32, (1, LANES), 1) < HEAD_DIM
    q_i = lax.broadcasted_iota(jnp.int32, (ATTN_BLOCK, ATTN_WINDOW), 0)
    k_j = lax.broadcasted_iota(jnp.int32, (ATTN_BLOCK, ATTN_WINDOW), 1)
    rel = q_i - k_j
    last = len(DILATED_PATTERNS) - 1

    def rotate(j, carry):
        rows = pl.ds(pl.multiple_of(j * ATTN_STAGE_ROWS, ATTN_STAGE_ROWS), ATTN_STAGE_ROWS)
        cos = cos_ref[rows, :]
        sin_lo = sin_lo_ref[rows, :]
        sin_hi = sin_hi_ref[rows, :]

        def rope(t):
            return (t * cos + pltpu.roll(t, LANES - HEAD_DIM // 2, 1) * sin_lo
                    + pltpu.roll(t, HEAD_DIM // 2, 1) * sin_hi)

        qr[rows, :] = rope(q_ref[rows, :]) * (HEAD_DIM ** -0.5)
        kr[rows, :] = rope(k_ref[rows, :])
        return carry

    lax.fori_loop(0, s_len // ATTN_STAGE_ROWS, rotate, 0, unroll=2)

    for pi, (window, dil) in enumerate(DILATED_PATTERNS):
        assert window // (2 * dil) == ATTN_HALF
        row_len = s_len // dil
        assert row_len % ATTN_WINDOW == 0 and row_len % ATTN_STAGE_ROWS == 0

        def stage(j, carry, dil=dil, row_len=row_len):
            f0 = pl.multiple_of(j * ATTN_STAGE_ROWS, ATTN_STAGE_ROWS)
            r = f0 // row_len
            src = _strided_rows((f0 - r * row_len) * dil + r, ATTN_STAGE_ROWS, dil)
            dst = pl.ds(f0, ATTN_STAGE_ROWS)
            qv = qr[src, :]
            q0s[dst, :] = jnp.where(head0, qv, 0.0).astype(jnp.bfloat16)
            q1s[dst, :] = jnp.where(head0, 0.0, qv).astype(jnp.bfloat16)
            ks[dst, :] = kr[src, :].astype(jnp.bfloat16)
            vs[dst, :] = v_ref[src, :].astype(jnp.bfloat16)
            return carry

        lax.fori_loop(0, s_len // ATTN_STAGE_ROWS, stage, 0, unroll=2)

        def block(bi, carry, pi=pi, dil=dil, row_len=row_len):
            f0 = pl.multiple_of(bi * ATTN_BLOCK, ATTN_BLOCK)
            r = f0 // row_len
            row_start = r * row_len
            win = jnp.clip(f0 - ATTN_HALF, row_start, row_start + row_len - ATTN_WINDOW)
            win = pl.multiple_of(win, ATTN_HALF)
            valid = jnp.abs(rel + (f0 - win)) <= ATTN_HALF
            kw = ks[pl.ds(win, ATTN_WINDOW), :]
            vw = vs[pl.ds(win, ATTN_WINDOW), :]
            parts = []
            for qs_ref in (q0s, q1s):
                qb = qs_ref[pl.ds(f0, ATTN_BLOCK), :]
                sc = lax.dot_general(qb, kw, (((1,), (1,)), ((), ())),
                                     preferred_element_type=jnp.float32)
                sc = jnp.where(valid, sc, -jnp.inf)
                m = jnp.max(sc, axis=-1, keepdims=True)
                p = jnp.exp(sc - m)
                den = jnp.sum(p, axis=-1, keepdims=True)
                num = jnp.dot(p.astype(jnp.bfloat16), vw, preferred_element_type=jnp.float32)
                parts.append((num, m, den))
            num = jnp.where(head0, parts[0][0], parts[1][0])
            m = jnp.where(head0, parts[0][1], parts[1][1])
            den = jnp.where(head0, parts[0][2], parts[1][2])
            rows = _strided_rows((f0 - row_start) * dil + r, ATTN_BLOCK, dil)
            if pi == 0:
                acc_o[rows, :] = num
                acc_m[rows, :] = m
                acc_l[rows, :] = den
            else:
                m_old = acc_m[rows, :]
                m_new = jnp.maximum(m_old, m)
                w_old = jnp.exp(m_old - m_new)
                w_cur = jnp.exp(m - m_new)
                num = acc_o[rows, :] * w_old + num * w_cur
                den = acc_l[rows, :] * w_old + den * w_cur
                if pi == last:
                    o_ref[rows, :] = num / den
                else:
                    acc_o[rows, :] = num
                    acc_m[rows, :] = m_new
                    acc_l[rows, :] = den
            return carry

        lax.fori_loop(0, s_len // ATTN_BLOCK, block, 0, unroll=4)


def _attention(proj):
    b, s, _ = proj.shape
    cos, sin = _rope_tables(s)
    reps = LANES // (HEAD_DIM // 2)
    first_half = (jnp.arange(LANES) % HEAD_DIM) < HEAD_DIM // 2
    cos_t = jnp.tile(cos, (1, reps))
    sin_t = jnp.tile(sin, (1, reps))
    sin_lo = jnp.where(first_half, -sin_t, 0.0)
    sin_hi = jnp.where(first_half, 0.0, sin_t)
    pairs = ATTN_WIDTH // LANES

    def cols(off):
        return pl.BlockSpec((None, s, LANES), lambda bi, hp, off=off: (bi, 0, off + hp))

    table = pl.BlockSpec((s, LANES), lambda bi, hp: (0, 0))
    return pl.pallas_call(
        _attention_kernel,
        grid=(b, pairs),
        in_specs=[cols(0), cols(pairs), cols(2 * pairs), table, table, table],
        out_specs=pl.BlockSpec((None, s, LANES), lambda bi, hp: (bi, 0, hp)),
        out_shape=jax.ShapeDtypeStruct((b, s, ATTN_WIDTH), jnp.float32),
        scratch_shapes=[pltpu.VMEM((s, LANES), jnp.float32)] * 2
        + [pltpu.VMEM((s, LANES), jnp.bfloat16)] * 4
        + [pltpu.VMEM((s, LANES), jnp.float32)] * 3,
        compiler_params=pltpu.CompilerParams(
            dimension_semantics=("arbitrary", "arbitrary"),
            vmem_limit_bytes=56 * 1024 * 1024),
        name="dilated_attention",
    )(proj, proj, proj, cos_t, sin_lo, sin_hi)


SSD_HALO = 8
SSD_GROUP_HEADS = SSD_HEADS // SSD_GROUPS
SSD_GROUP_WIDTH = SSD_GROUP_HEADS * SSD_HEAD_DIM


def _softplus(v):
    return jnp.maximum(v, 0.0) + jnp.log(1.0 + jnp.exp(-jnp.abs(v)))


def _expand_heads(mat, first_head):
    t = mat.shape[0]
    low = lax.broadcasted_iota(jnp.int32, (1, LANES), 1) < SSD_HEAD_DIM
    tiles = []
    for j in range(0, SSD_GROUP_HEADS, 2):
        a = jnp.broadcast_to(mat[:, first_head + j:first_head + j + 1], (t, LANES))
        b = jnp.broadcast_to(mat[:, first_head + j + 1:first_head + j + 2], (t, LANES))
        tiles.append(jnp.where(low, a, b))
    return jnp.concatenate(tiles, axis=-1)


def _ssd_kernel(xbc_ref, prev_ref, next_ref, dt_ref, cw_ref, cb_ref, dtb_ref, alog_ref, dsk_ref,
                y_ref, ext, cv, state, *, reverse):
    i = pl.program_id(1)
    nc = pl.num_programs(1)
    c = (nc - 1 - i) if reverse else i
    t = SSD_CHUNK
    bf16 = jnp.bfloat16

    @pl.when(i == 0)
    def _():
        state[...] = jnp.zeros_like(state)

    ext[0:SSD_HALO, :] = jnp.where(c > 0, prev_ref[...], 0.0)
    ext[SSD_HALO:SSD_HALO + t, :] = xbc_ref[...]
    ext[SSD_HALO + t:, :] = jnp.where(c < nc - 1, next_ref[...], 0.0)
    for j in range(SSD_CONV_CH // LANES):
        cols = slice(j * LANES, (j + 1) * LANES)
        acc = jnp.broadcast_to(cb_ref[:, cols], (t, LANES))
        for k in range(SSD_CONV):
            acc = acc + cw_ref[k:k + 1, cols] * ext[pl.ds(SSD_HALO - SSD_CONV // 2 + k, t), cols]
        cv[:, cols] = acc * jax.nn.sigmoid(acc)

    dt = _softplus(dt_ref[...] + dtb_ref[...])
    a = dt * (-jnp.exp(alog_ref[...]))
    row = lax.broadcasted_iota(jnp.int32, (t, LANES), 0)
    lane = lax.broadcasted_iota(jnp.int32, (t, LANES), 1)
    cs = a
    shift = 1
    while shift < t:
        cs = cs + jnp.where(row >= shift, pltpu.roll(cs, shift, 0), 0.0)
        shift *= 2
    total = cs[t - 1:t, :]
    e = jnp.where(lane < SSD_HEADS, cs, total - cs + a)
    off = SSD_HEADS if reverse else 0
    edge = e[0:1, :] if reverse else e[t - 1:t, :]
    scale_y = jnp.exp(e)
    scale_s = jnp.exp(edge - e) * dt

    x_off = 0
    b_off = SSD_WIDTH
    c_off = SSD_WIDTH + SSD_GROUPS * SSD_STATE
    low = lax.broadcasted_iota(jnp.int32, (1, LANES), 1) < SSD_HEAD_DIM

    if not reverse:
        e_t = e.T
        dt_t = dt.T
        li = lax.broadcasted_iota(jnp.int32, (t, t), 0)
        si = lax.broadcasted_iota(jnp.int32, (t, t), 1)
        lower = si <= li
        upper = si >= li

    for g in range(SSD_GROUPS):
        bg = cv[:, b_off + g * SSD_STATE:b_off + (g + 1) * SSD_STATE].astype(bf16)
        cg = cv[:, c_off + g * SSD_STATE:c_off + (g + 1) * SSD_STATE].astype(bf16)
        xg = cv[:, x_off + g * SSD_GROUP_WIDTH:x_off + (g + 1) * SSD_GROUP_WIDTH]
        h0 = g * SSD_GROUP_HEADS
        st = state[g]
        y = jnp.dot(cg, st.astype(bf16), preferred_element_type=jnp.float32) \
            * _expand_heads(scale_y, off + h0)
        xw = (xg * _expand_heads(scale_s, off + h0)).astype(bf16)
        new = lax.dot_general(bg, xw, (((0,), (0,)), ((), ())),
                              preferred_element_type=jnp.float32)
        decay = _expand_heads(jnp.broadcast_to(jnp.exp(edge), (SSD_HALO, LANES)), off + h0)
        state[g] = st * decay[0:1, :] + new

        if not reverse:
            cbm = lax.dot_general(cg, bg, (((1,), (1,)), ((), ())),
                                  preferred_element_type=jnp.float32)
            pieces = []
            for j in range(0, SSD_GROUP_HEADS, 2):
                xp = xg[:, j * SSD_HEAD_DIM:(j + 2) * SSD_HEAD_DIM]
                yp = None
                for hh, keep in ((j, low), (j + 1, jnp.logical_not(low))):
                    h = h0 + hh
                    hb = SSD_HEADS + h
                    lf = jnp.exp(jnp.where(lower, e[:, h:h + 1] - e_t[h:h + 1, :], -jnp.inf))
                    lb = jnp.exp(jnp.where(upper, e[:, hb:hb + 1] - e_t[hb:hb + 1, :], -jnp.inf))
                    mh = cbm * (lf * dt_t[h:h + 1, :] + lb * dt_t[hb:hb + 1, :])
                    part = jnp.dot(mh.astype(bf16), jnp.where(keep, xp, 0.0).astype(bf16),
                                   preferred_element_type=jnp.float32)
                    yp = part if yp is None else yp + part
                pieces.append(yp)
            gcols = slice(g * SSD_GROUP_WIDTH, (g + 1) * SSD_GROUP_WIDTH)
            y = y + jnp.concatenate(pieces, axis=-1) + dsk_ref[:, gcols] * xg
        y_ref[:, g * SSD_GROUP_WIDTH:(g + 1) * SSD_GROUP_WIDTH] = y


def _ssd_scan(proj, conv_w, conv_b, dt_bias, a_log, d_skip, reverse):
    b, s, _ = proj.shape
    nc = s // SSD_CHUNK
    z_end = 3 * ATTN_WIDTH + SSD_WIDTH
    xbc_blk = z_end // SSD_CONV_CH
    dt_blk = (z_end + SSD_CONV_CH) // LANES
    assert z_end % SSD_CONV_CH == 0 and (z_end + SSD_CONV_CH) % LANES == 0
    halo_per_chunk = SSD_CHUNK // SSD_HALO
    n_halo = s // SSD_HALO

    def chunk(i):
        return (nc - 1 - i) if reverse else i

    cw = jnp.pad(conv_w, ((0, SSD_HALO - SSD_CONV), (0, 0)))
    pad16 = lambda v: jnp.pad(v, (0, LANES - 2 * SSD_HEADS)).reshape(1, LANES)
    vec = lambda width: pl.BlockSpec((1, width), lambda bi, i: (0, 0))
    return pl.pallas_call(
        functools.partial(_ssd_kernel, reverse=reverse),
        grid=(b, nc),
        in_specs=[
            pl.BlockSpec((None, SSD_CHUNK, SSD_CONV_CH), lambda bi, i: (bi, chunk(i), xbc_blk)),
            pl.BlockSpec((None, SSD_HALO, SSD_CONV_CH),
                         lambda bi, i: (bi, jnp.maximum(chunk(i) * halo_per_chunk - 1, 0), xbc_blk)),
            pl.BlockSpec((None, SSD_HALO, SSD_CONV_CH),
                         lambda bi, i: (bi, jnp.minimum((chunk(i) + 1) * halo_per_chunk, n_halo - 1),
                                        xbc_blk)),
            pl.BlockSpec((None, SSD_CHUNK, LANES), lambda bi, i: (bi, chunk(i), dt_blk)),
            pl.BlockSpec((SSD_HALO, SSD_CONV_CH), lambda bi, i: (0, 0)),
            vec(SSD_CONV_CH), vec(LANES), vec(LANES), vec(SSD_WIDTH),
        ],
        out_specs=pl.BlockSpec((None, SSD_CHUNK, SSD_WIDTH), lambda bi, i: (bi, chunk(i), 0)),
        out_shape=jax.ShapeDtypeStruct((b, s, SSD_WIDTH), jnp.float32),
        scratch_shapes=[
            pltpu.VMEM((SSD_CHUNK + 2 * SSD_HALO, SSD_CONV_CH), jnp.float32),
            pltpu.VMEM((SSD_CHUNK, SSD_CONV_CH), jnp.float32),
            pltpu.VMEM((SSD_GROUPS, SSD_STATE, SSD_GROUP_WIDTH), jnp.float32),
        ],
        compiler_params=pltpu.CompilerParams(
            dimension_semantics=("arbitrary", "arbitrary"),
            vmem_limit_bytes=32 * 1024 * 1024),
        name="ssd_bwd" if reverse else "ssd_fwd",
    )(proj, proj, proj, proj, cw, conv_b.reshape(1, SSD_CONV_CH), pad16(dt_bias), pad16(a_log),
      jnp.repeat(d_skip, SSD_HEAD_DIM).reshape(1, SSD_WIDTH))


def _ssd_pre_gate(proj, conv_w, conv_b, dt_bias_fwd, a_log_fwd, dt_bias_bwd, a_log_bwd, d_skip):
    dt_bias = jnp.concatenate([dt_bias_fwd, dt_bias_bwd])
    a_log = jnp.concatenate([a_log_fwd, a_log_bwd])
    y_f = _ssd_scan(proj, conv_w, conv_b, dt_bias, a_log, d_skip, reverse=False)
    y_b = _ssd_scan(proj, conv_w, conv_b, dt_bias, a_log, d_skip, reverse=True)
    return y_f, y_b


def _mix_kernel(attn_ref, yf_ref, yb_ref, z_ref, x_ref, anw_ref, snw_ref, wout_ref,
                l1w_ref, l1b_ref, rw_hi_ref, rw_lo_ref, rb_ref, h_ref, e_ref, g_ref):
    bf16 = jnp.bfloat16
    a = attn_ref[...]
    a = a * lax.rsqrt(jnp.mean(a * a, axis=-1, keepdims=True) + NORM_EPS) * anw_ref[...]
    z = z_ref[...]
    y = (yf_ref[...] + yb_ref[...]) * (z * jax.nn.sigmoid(z))
    groups = []
    for g in range(SSD_GROUPS):
        yg = y[:, g * SSD_GROUP_WIDTH:(g + 1) * SSD_GROUP_WIDTH]
        groups.append(yg * lax.rsqrt(jnp.mean(yg * yg, axis=-1, keepdims=True) + NORM_EPS))
    y = jnp.concatenate(groups, axis=-1) * snw_ref[...]
    mix = (jnp.dot(a.astype(bf16), wout_ref[0:ATTN_WIDTH, :], preferred_element_type=jnp.float32)
           + jnp.dot(y.astype(bf16), wout_ref[ATTN_WIDTH:, :], preferred_element_type=jnp.float32))
    v = DEEPNORM_ALPHA * x_ref[...] + mix
    mu = jnp.mean(v, axis=-1, keepdims=True)
    var = jnp.mean(jnp.square(v - mu), axis=-1, keepdims=True)
    h1 = (v - mu) * lax.rsqrt(var + NORM_EPS) * l1w_ref[...] + l1b_ref[...]
    for c in range(SLAB):
        h_ref[pl.ds(c, MIX_ROWS, stride=SLAB), :] = h1[:, c * LANES:(c + 1) * LANES]

    h_hi = h1.astype(bf16)
    h_lo = (h1 - h_hi.astype(jnp.float32)).astype(bf16)
    logits = (jnp.dot(h_hi, rw_hi_ref[...], preferred_element_type=jnp.float32)
              + jnp.dot(h_lo, rw_hi_ref[...], preferred_element_type=jnp.float32)
              + jnp.dot(h_hi, rw_lo_ref[...], preferred_element_type=jnp.float32)
              + rb_ref[...])
    lane = lax.broadcasted_iota(jnp.int32, (MIX_ROWS, LANES), 1)
    lane_f = lane.astype(jnp.float32)
    cur = jnp.where(lane < N_EXPERTS, logits, -jnp.inf)
    vals, idxs = [], []
    for _ in range(TOP_K):
        m = jnp.max(cur, axis=-1, keepdims=True)
        idx = jnp.min(jnp.where(cur == m, lane_f, float(LANES)), axis=-1, keepdims=True)
        vals.append(m)
        idxs.append(idx)
        cur = jnp.where(lane_f == idx, -jnp.inf, cur)
    probs = [jnp.exp(val - vals[0]) for val in vals]
    den = probs[0]
    for p in probs[1:]:
        den = den + p
    e_out = jnp.zeros((MIX_ROWS, LANES), jnp.float32)
    g_out = jnp.zeros((MIX_ROWS, LANES), jnp.float32)
    for k in range(TOP_K):
        e_out = jnp.where(lane == k, idxs[k], e_out)
        g_out = jnp.where(lane == k, probs[k] / den, g_out)
    e_ref[...] = e_out.astype(jnp.int32)
    g_ref[...] = g_out


def _mix_ln_router(attn, y_f, y_b, proj, x, attn_norm_w, ssd_norm_w, w_out, ln1_w, ln1_b,
                   router_w, router_b):
    n, d = x.shape
    z_blk = (3 * ATTN_WIDTH) // SSD_WIDTH
    rw = jnp.pad(router_w, ((0, 0), (0, LANES - N_EXPERTS)))
    rw_hi = rw.astype(jnp.bfloat16)
    rw_lo = (rw - rw_hi.astype(jnp.float32)).astype(jnp.bfloat16)
    rb = jnp.pad(router_b, (0, LANES - N_EXPERTS)).reshape(1, LANES)
    half = pl.BlockSpec((MIX_ROWS, ATTN_WIDTH), lambda i: (i, 0))
    vec = lambda width: pl.BlockSpec((1, width), lambda i: (0, 0))
    full = lambda r, c: pl.BlockSpec((r, c), lambda i: (0, 0))
    return pl.pallas_call(
        _mix_kernel,
        grid=(n // MIX_ROWS,),
        in_specs=[half, half, half,
                  pl.BlockSpec((MIX_ROWS, SSD_WIDTH), lambda i: (i, z_blk)),
                  pl.BlockSpec((MIX_ROWS, d), lambda i: (i, 0)),
                  vec(ATTN_WIDTH), vec(SSD_WIDTH), full(d, d), vec(d), vec(d),
                  full(d, LANES), full(d, LANES), vec(LANES)],
        out_specs=[pl.BlockSpec((MIX_ROWS * SLAB, LANES), lambda i: (i, 0)),
                   pl.BlockSpec((MIX_ROWS, LANES), lambda i: (i, 0)),
                   pl.BlockSpec((MIX_ROWS, LANES), lambda i: (i, 0))],
        out_shape=[jax.ShapeDtypeStruct((n * SLAB, LANES), jnp.float32),
                   jax.ShapeDtypeStruct((n, LANES), jnp.int32),
                   jax.ShapeDtypeStruct((n, LANES), jnp.float32)],
        compiler_params=pltpu.CompilerParams(
            dimension_semantics=("arbitrary",),
            vmem_limit_bytes=48 * 1024 * 1024),
        name="mix_ln_router",
    )(attn, y_f, y_b, proj, x, attn_norm_w.reshape(1, ATTN_WIDTH), ssd_norm_w.reshape(1, SSD_WIDTH),
      w_out.astype(jnp.bfloat16), ln1_w.reshape(1, d), ln1_b.reshape(1, d), rw_hi, rw_lo, rb)


MOE_GATHER_PER_DOT = MOE_TILE // (2 * (D_MODEL // FFN_CHUNK))
MOE_SCATTER_EARLY = MOE_GATHER_PER_DOT // 2
MOE_SCATTER_LATE = (MOE_TILE - 2 * (D_MODEL // FFN_CHUNK) * MOE_SCATTER_EARLY) \
    // (D_MODEL // FFN_CHUNK)


def _moe_kernel(te_ref, nt_ref,
                src_cur_ref, src_nxt_ref, dst_prv_ref,
                h_hbm, wg_ref, wu_ref, wd_ref, bg_ref, bu_ref, bd_ref,
                y_hbm,
                xbuf, ybuf, xb_s, act_s, wg_bf, wu_bf, wd_bf, gsem, ssem):
    i = pl.program_id(0)
    n_tiles = nt_ref[0]
    slot = i % 2
    other = 1 - slot
    n_pairs = y_hbm.shape[0] // SLAB - 2 * MOE_TILE
    bf16 = jnp.bfloat16

    def slab(row):
        return pl.ds(pl.multiple_of(row * SLAB, SLAB), SLAB)

    def gather_copy(tok, row, s):
        return pltpu.make_async_copy(h_hbm.at[slab(tok), :],
                                     xbuf.at[s, slab(row), :], gsem.at[s])

    def scatter_copy(dst, row, s):
        return pltpu.make_async_copy(ybuf.at[s, slab(row), :],
                                     y_hbm.at[slab(dst), :], ssem.at[s])

    def prev_dst(r):
        return jnp.where(i == 0, n_pairs + MOE_TILE + r, dst_prv_ref[0, 0, r])

    def wait_gather(s):
        def body(r, c):
            gather_copy(0, r, s).wait()
            return c
        lax.fori_loop(0, MOE_TILE, body, 0, unroll=8)

    def wait_scatter(s):
        def body(r, c):
            scatter_copy(0, r, s).wait()
            return c
        lax.fori_loop(0, MOE_TILE, body, 0, unroll=8)

    @pl.when(i == 0)
    def _first():
        ybuf[...] = jnp.zeros_like(ybuf)
        for half in range(2):
            cp = pltpu.make_async_copy(
                ybuf.at[0],
                y_hbm.at[pl.ds((n_pairs + half * MOE_TILE) * SLAB, MOE_TILE * SLAB), :],
                ssem.at[half])
            cp.start()
            cp.wait()

        def body(r, c):
            gather_copy(src_cur_ref[0, 0, r], r, 0).start()
            return c
        lax.fori_loop(0, MOE_TILE, body, 0, unroll=8)

    @pl.when(i <= n_tiles)
    def _arrive():
        wait_gather(slot)

        @pl.when(i >= 1)
        def _():
            wait_scatter(slot)

    @pl.when(i < n_tiles)
    def _compute():
        e = te_ref[i]
        prev_e = te_ref[jnp.maximum(i - 1, 0)]

        @pl.when((i == 0) | (e != prev_e))
        def _():
            wg_bf[...] = wg_ref[0].astype(bf16)
            wu_bf[...] = wu_ref[0].astype(bf16)
            wd_bf[...] = wd_ref[0].astype(bf16)

        gather_done = [0]
        scatter_done = [0]

        def issue(n_gather, n_scatter):
            for r in range(gather_done[0], gather_done[0] + n_gather):
                gather_copy(src_nxt_ref[0, 0, r], r, other).start()
            for r in range(scatter_done[0], scatter_done[0] + n_scatter):
                scatter_copy(prev_dst(r), r, other).start()
            gather_done[0] += n_gather
            scatter_done[0] += n_scatter

        for c in range(SLAB):
            xb_s[:, c * LANES:(c + 1) * LANES] = \
                xbuf[slot, pl.ds(c, MOE_TILE, stride=SLAB), :].astype(bf16)

        for c in range(D_MODEL // FFN_CHUNK):
            cols = slice(c * FFN_CHUNK, (c + 1) * FFN_CHUNK)
            g = jnp.dot(xb_s[...], wg_bf[:, cols], preferred_element_type=jnp.float32)
            g = g + bg_ref[pl.ds(e, 1), cols]
            issue(MOE_GATHER_PER_DOT, MOE_SCATTER_EARLY)
            u = jnp.dot(xb_s[...], wu_bf[:, cols], preferred_element_type=jnp.float32)
            u = u + bu_ref[pl.ds(e, 1), cols]
            issue(MOE_GATHER_PER_DOT, MOE_SCATTER_EARLY)
            g = jnp.minimum(g, SWIGLU_LIMIT)
            u = jnp.clip(u, -SWIGLU_LIMIT, SWIGLU_LIMIT)
            act_s[:, cols] = ((u + 1.0) * g * jax.nn.sigmoid(SWIGLU_ALPHA * g)).astype(bf16)

        for c in range(D_MODEL // FFN_CHUNK):
            cols = slice(c * FFN_CHUNK, (c + 1) * FFN_CHUNK)
            y = jnp.dot(act_s[...], wd_bf[:, cols], preferred_element_type=jnp.float32)
            y = y + bd_ref[pl.ds(e, 1), cols]
            for j in range(FFN_CHUNK // LANES):
                ybuf[slot, pl.ds(c * (FFN_CHUNK // LANES) + j, MOE_TILE, stride=SLAB), :] = \
                    y[:, j * LANES:(j + 1) * LANES]
            issue(0, MOE_SCATTER_LATE)
        assert gather_done[0] == MOE_TILE and scatter_done[0] == MOE_TILE

    @pl.when(i == n_tiles)
    def _drain():
        def body(r, c):
            scatter_copy(prev_dst(r), r, other).start()
            return c
        lax.fori_loop(0, MOE_TILE, body, 0, unroll=8)
        wait_scatter(other)


def _moe_ffn(h1_slab, tile_e, n_tiles, src_tok, dst_row, w_gate, b_gate, w_up, b_up,
             w_down, b_down, n_out_rows):
    t_max = tile_e.shape[0]
    d = D_MODEL

    def idx_spec(shift):
        return pl.BlockSpec((1, 1, MOE_TILE),
                            lambda i, te, nt: (jnp.maximum(i + shift, 0), 0, 0),
                            memory_space=pltpu.SMEM)

    w_spec = pl.BlockSpec((1, d, d), lambda i, te, nt: (te[i], 0, 0))
    b_spec = pl.BlockSpec((N_EXPERTS, d), lambda i, te, nt: (0, 0))
    grid_spec = pltpu.PrefetchScalarGridSpec(
        num_scalar_prefetch=2,
        grid=(t_max,),
        in_specs=[idx_spec(0), idx_spec(1), idx_spec(-1),
                  pl.BlockSpec(memory_space=pl.ANY),
                  w_spec, w_spec, w_spec, b_spec, b_spec, b_spec],
        out_specs=pl.BlockSpec(memory_space=pl.ANY),
        scratch_shapes=[
            pltpu.VMEM((2, MOE_TILE * SLAB, LANES), jnp.float32),
            pltpu.VMEM((2, MOE_TILE * SLAB, LANES), jnp.float32),
            pltpu.VMEM((MOE_TILE, d), jnp.bfloat16),
            pltpu.VMEM((MOE_TILE, d), jnp.bfloat16),
            pltpu.VMEM((d, d), jnp.bfloat16),
            pltpu.VMEM((d, d), jnp.bfloat16),
            pltpu.VMEM((d, d), jnp.bfloat16),
            pltpu.SemaphoreType.DMA((2,)),
            pltpu.SemaphoreType.DMA((2,)),
        ],
    )
    return pl.pallas_call(
        _moe_kernel,
        grid_spec=grid_spec,
        out_shape=jax.ShapeDtypeStruct((n_out_rows * SLAB, LANES), jnp.float32),
        compiler_params=pltpu.CompilerParams(
            dimension_semantics=("arbitrary",),
            vmem_limit_bytes=V7X_VMEM_BYTES - 6 * 1024 * 1024),
        name="moe_ffn",
    )(tile_e, n_tiles, src_tok, src_tok, dst_row, h1_slab,
      w_gate, w_up, w_down, b_gate, b_up, b_down)


def _combine_ln_kernel(y0_ref, y1_ref, y2_ref, y3_ref, h_ref, g_ref, w_ref, b_ref, o_ref):
    g = g_ref[...]
    ffn = (g[:, 0:1] * _slab_rows(y0_ref, LN_ROWS) + g[:, 1:2] * _slab_rows(y1_ref, LN_ROWS)
           + g[:, 2:3] * _slab_rows(y2_ref, LN_ROWS) + g[:, 3:4] * _slab_rows(y3_ref, LN_ROWS))
    v = DEEPNORM_ALPHA * _slab_rows(h_ref, LN_ROWS) + ffn
    mu = jnp.mean(v, axis=-1, keepdims=True)
    var = jnp.mean(jnp.square(v - mu), axis=-1, keepdims=True)
    o_ref[...] = (v - mu) * lax.rsqrt(var + NORM_EPS) * w_ref[...] + b_ref[...]


def _combine_ln(y_planes, h1_slab, gates, ln_w, ln_b):
    n = gates.shape[0]
    d = D_MODEL
    nb = n // LN_ROWS

    def plane(k):
        return pl.BlockSpec((LN_ROWS * SLAB, LANES), lambda i, k=k: (k * nb + i, 0))

    vec = pl.BlockSpec((1, d), lambda i: (0, 0))
    return pl.pallas_call(
        _combine_ln_kernel,
        grid=(nb,),
        in_specs=[plane(0), plane(1), plane(2), plane(3), plane(0),
                  pl.BlockSpec((LN_ROWS, LANES), lambda i: (i, 0)), vec, vec],
        out_specs=pl.BlockSpec((LN_ROWS, d), lambda i: (i, 0)),
        out_shape=jax.ShapeDtypeStruct((n, d), jnp.float32),
        compiler_params=pltpu.CompilerParams(
            dimension_semantics=("arbitrary",),
            vmem_limit_bytes=48 * 1024 * 1024),
        name="combine_ln",
    )(y_planes, y_planes, y_planes, y_planes, h1_slab, gates,
      ln_w.reshape(1, d), ln_b.reshape(1, d))


def _routing_tables(top_idx, n_tok):
    n_pairs = TOP_K * n_tok
    t_max = n_pairs // MOE_TILE + N_EXPERTS + 1
    flat_e = top_idx.T.reshape(-1).astype(jnp.int32)
    order = jnp.argsort(flat_e, stable=True).astype(jnp.int32)
    counts = jnp.sum((flat_e[:, None] == jnp.arange(N_EXPERTS, dtype=jnp.int32)[None, :])
                     .astype(jnp.int32), axis=0)
    tiles_per_e = (counts + MOE_TILE - 1) // MOE_TILE
    tile_end = jnp.cumsum(tiles_per_e)
    tile_start = tile_end - tiles_per_e
    n_tiles = tile_end[-1]
    row_start = jnp.cumsum(counts) - counts
    tile_ids = jnp.arange(t_max + 1, dtype=jnp.int32)
    tile_e = jnp.searchsorted(tile_end, jnp.minimum(tile_ids, n_tiles - 1), side="right")
    tile_e = jnp.minimum(tile_e, N_EXPERTS - 1).astype(jnp.int32)
    lane = jnp.arange(MOE_TILE, dtype=jnp.int32)[None, :]
    row_in_e = (tile_ids - tile_start[tile_e])[:, None] * MOE_TILE + lane
    valid = (row_in_e < counts[tile_e][:, None]) & (tile_ids < n_tiles)[:, None]
    j = jnp.clip(row_start[tile_e][:, None] + row_in_e, 0, n_pairs - 1)
    pair = order[j]
    src_tok = jnp.where(valid, pair % n_tok, 0).astype(jnp.int32)
    spare = n_pairs + (tile_ids % 2)[:, None] * MOE_TILE + lane
    dst_row = jnp.where(valid, pair, spare).astype(jnp.int32)
    return (tile_e[:t_max], n_tiles.reshape(1).astype(jnp.int32),
            src_tok.reshape(t_max + 1, 1, MOE_TILE), dst_row.reshape(t_max + 1, 1, MOE_TILE))


def kernel(x, w_in, attn_norm_w, conv_w, conv_b, dt_bias_fwd, a_log_fwd, dt_bias_bwd,
           a_log_bwd, d_skip, ssd_norm_w, w_out, ln1_w, ln1_b, router_w, router_b,
           w_gate, b_gate, w_up, b_up, w_down, b_down, ln2_w, ln2_b):
    b, s, d = x.shape
    n = b * s
    xf = x.reshape(n, d)
    in_cols = w_in.shape[-1]
    in_cols_pad = -(-in_cols // LANES) * LANES

    w_in_b = jnp.pad(w_in[0], ((0, 0), (0, in_cols_pad - in_cols))).astype(jnp.bfloat16)
    proj = _rows_matmul(xf, w_in_b, "in_proj")
    proj3 = proj.reshape(b, s, in_cols_pad)
    attn = _attention(proj3)
    y_f, y_b = _ssd_pre_gate(proj3, conv_w[0], conv_b[0], dt_bias_fwd[0], a_log_fwd[0],
                             dt_bias_bwd[0], a_log_bwd[0], d_skip[0])
    h1_slab, top_e, gates = _mix_ln_router(
        attn.reshape(n, ATTN_WIDTH), y_f.reshape(n, SSD_WIDTH), y_b.reshape(n, SSD_WIDTH),
        proj, xf, attn_norm_w[0], ssd_norm_w[0], w_out[0], ln1_w[0], ln1_b[0],
        router_w[0], router_b[0])

    tile_e, n_tiles, src_tok, dst_row = _routing_tables(top_e[:, :TOP_K], n)
    n_out_rows = TOP_K * n + 2 * MOE_TILE
    y_planes = _moe_ffn(h1_slab, tile_e, n_tiles, src_tok, dst_row,
                        w_gate[0], b_gate[0], w_up[0], b_up[0], w_down[0], b_down[0],
                        n_out_rows)
    out = _combine_ln(y_planes, h1_slab, gates, ln2_w[0], ln2_b[0])
    return out.reshape(b, s, d)
```

```python
import functools

import jax
import jax.numpy as jnp
from jax import lax
from jax.experimental import pallas as pl
from jax.experimental.pallas import tpu as pltpu

D_MODEL = 1024
HEAD_DIM = 64
ATTN_HEADS = 8
ATTN_WIDTH = ATTN_HEADS * HEAD_DIM
DILATED_PATTERNS = ((128, 1), (512, 4), (2048, 16))
ROPE_THETA = 10000.0
SSD_HEADS = 8
SSD_HEAD_DIM = 64
SSD_WIDTH = SSD_HEADS * SSD_HEAD_DIM
SSD_GROUPS = 2
SSD_STATE = 128
SSD_CONV = 5
SSD_CHUNK = 128
SSD_CONV_CH = SSD_WIDTH + 2 * SSD_GROUPS * SSD_STATE
N_EXPERTS = 32
TOP_K = 4
SWIGLU_ALPHA = 1.702
SWIGLU_LIMIT = 7.0
DEPTH = 1
DEEPNORM_ALPHA = (2.0 * DEPTH) ** 0.25
NORM_EPS = 1e-5

V7X_VMEM_BYTES = 64 * 1024 * 1024
LANES = 128

MATMUL_ROWS = 512
MIX_ROWS = 512
MOE_TILE = 512
FFN_CHUNK = 256
LN_ROWS = 512
SLAB = D_MODEL // LANES


def _slab_rows(ref, rows):
    return jnp.concatenate(
        [ref[pl.ds(c, rows, stride=SLAB), :] for c in range(SLAB)], axis=-1)


def _rows_matmul_kernel(x_ref, w_ref, o_ref):
    o_ref[...] = jnp.dot(x_ref[...].astype(jnp.bfloat16), w_ref[...],
                         preferred_element_type=jnp.float32)


def _rows_matmul(x, w_bf16, name):
    n, k = x.shape
    _, m = w_bf16.shape
    assert n % MATMUL_ROWS == 0 and m % LANES == 0
    return pl.pallas_call(
        _rows_matmul_kernel,
        grid=(n // MATMUL_ROWS,),
        in_specs=[pl.BlockSpec((MATMUL_ROWS, k), lambda i: (i, 0)),
                  pl.BlockSpec((k, m), lambda i: (0, 0))],
        out_specs=pl.BlockSpec((MATMUL_ROWS, m), lambda i: (i, 0)),
        out_shape=jax.ShapeDtypeStruct((n, m), jnp.float32),
        compiler_params=pltpu.CompilerParams(
            dimension_semantics=("arbitrary",),
            vmem_limit_bytes=48 * 1024 * 1024),
        name=name,
    )(x, w_bf16)


ATTN_BLOCK = 128
ATTN_WINDOW = 256
ATTN_HALF = 64
ATTN_STAGE_ROWS = 256


def _rope_tables(s):
    pos = jnp.arange(s, dtype=jnp.float32)
    inv_freq = ROPE_THETA ** (-jnp.arange(0, HEAD_DIM, 2, dtype=jnp.float32) / HEAD_DIM)
    ang = pos[:, None] * inv_freq[None, :]
    return jnp.cos(ang), jnp.sin(ang)


def _strided_rows(start, size, stride):
    return pl.ds(start, size) if stride == 1 else pl.ds(start, size, stride=stride)


def _attention_kernel(q_ref, k_ref, v_ref, cos_ref, sin_lo_ref, sin_hi_ref, o_ref,
                      qr, kr, q0s, q1s, ks, vs, acc_o, acc_m, acc_l):
    s_len = q_ref.shape[0]
    head0 = lax.broadcasted_iota(jnp.int32, (1, LANES), 1) < HEAD_DIM
    q_i = lax.broadcasted_iota(jnp.int32, (ATTN_BLOCK, ATTN_WINDOW), 0)
    k_j = lax.broadcasted_iota(jnp.int32, (ATTN_BLOCK, ATTN_WINDOW), 1)
    rel = q_i - k_j
    last = len(DILATED_PATTERNS) - 1

    def rotate(j, carry):
        rows = pl.ds(pl.multiple_of(j * ATTN_STAGE_ROWS, ATTN_STAGE_ROWS), ATTN_STAGE_ROWS)
        cos = cos_ref[rows, :]
        sin_lo = sin_lo_ref[rows, :]
        sin_hi = sin_hi_ref[rows, :]

        def rope(t):
            return (t * cos + pltpu.roll(t, LANES - HEAD_DIM // 2, 1) * sin_lo
                    + pltpu.roll(t, HEAD_DIM // 2, 1) * sin_hi)

        qr[rows, :] = rope(q_ref[rows, :]) * (HEAD_DIM ** -0.5)
        kr[rows, :] = rope(k_ref[rows, :])
        return carry

    lax.fori_loop(0, s_len // ATTN_STAGE_ROWS, rotate, 0, unroll=2)

    for pi, (window, dil) in enumerate(DILATED_PATTERNS):
        assert window // (2 * dil) == ATTN_HALF
        row_len = s_len // dil
        assert row_len % ATTN_WINDOW == 0 and row_len % ATTN_STAGE_ROWS == 0

        def stage(j, carry, dil=dil, row_len=row_len):
            f0 = pl.multiple_of(j * ATTN_STAGE_ROWS, ATTN_STAGE_ROWS)
            r = f0 // row_len
            src = _strided_rows((f0 - r * row_len) * dil + r, ATTN_STAGE_ROWS, dil)
            dst = pl.ds(f0, ATTN_STAGE_ROWS)
            qv = qr[src, :]
            q0s[dst, :] = jnp.where(head0, qv, 0.0).astype(jnp.bfloat16)
            q1s[dst, :] = jnp.where(head0, 0.0, qv).astype(jnp.bfloat16)
            ks[dst, :] = kr[src, :].astype(jnp.bfloat16)
            vs[dst, :] = v_ref[src, :].astype(jnp.bfloat16)
            return carry

        lax.fori_loop(0, s_len // ATTN_STAGE_ROWS, stage, 0, unroll=2)

        def block(bi, carry, pi=pi, dil=dil, row_len=row_len):
            f0 = pl.multiple_of(bi * ATTN_BLOCK, ATTN_BLOCK)
            r = f0 // row_len
            row_start = r * row_len
            win = jnp.clip(f0 - ATTN_HALF, row_start, row_start + row_len - ATTN_WINDOW)
            win = pl.multiple_of(win, ATTN_HALF)
            valid = jnp.abs(rel + (f0 - win)) <= ATTN_HALF
            kw = ks[pl.ds(win, ATTN_WINDOW), :]
            vw = vs[pl.ds(win, ATTN_WINDOW), :]
            parts = []
            for qs_ref in (q0s, q1s):
                qb = qs_ref[pl.ds(f0, ATTN_BLOCK), :]
                sc = lax.dot_general(qb, kw, (((1,), (1,)), ((), ())),
                                     preferred_element_type=jnp.float32)
                sc = jnp.where(valid, sc, -jnp.inf)
                m = jnp.max(sc, axis=-1, keepdims=True)
                p = jnp.exp(sc - m)
                den = jnp.sum(p, axis=-1, keepdims=True)
                num = jnp.dot(p.astype(jnp.bfloat16), vw, preferred_element_type=jnp.float32)
                parts.append((num, m, den))
            num = jnp.where(head0, parts[0][0], parts[1][0])
            m = jnp.where(head0, parts[0][1], parts[1][1])
            den = jnp.where(head0, parts[0][2], parts[1][2])
            rows = _strided_rows((f0 - row_start) * dil + r, ATTN_BLOCK, dil)
            if pi == 0:
                acc_o[rows, :] = num
                acc_m[rows, :] = m
                acc_l[rows, :] = den
            else:
                m_old = acc_m[rows, :]
                m_new = jnp.maximum(m_old, m)
                w_old = jnp.exp(m_old - m_new)
                w_cur = jnp.exp(m - m_new)
                num = acc_o[rows, :] * w_old + num * w_cur
                den = acc_l[rows, :] * w_old + den * w_cur
                if pi == last:
                    o_ref[rows, :] = num / den
                else:
                    acc_o[rows, :] = num
                    acc_m[rows, :] = m_new
                    acc_l[rows, :] = den
            return carry

        lax.fori_loop(0, s_len // ATTN_BLOCK, block, 0, unroll=4)


def _attention(proj):
    b, s, _ = proj.shape
    cos, sin = _rope_tables(s)
    reps = LANES // (HEAD_DIM // 2)
    first_half = (jnp.arange(LANES) % HEAD_DIM) < HEAD_DIM // 2
    cos_t = jnp.tile(cos, (1, reps))
    sin_t = jnp.tile(sin, (1, reps))
    sin_lo = jnp.where(first_half, -sin_t, 0.0)
    sin_hi = jnp.where(first_half, 0.0, sin_t)
    pairs = ATTN_WIDTH // LANES

    def cols(off):
        return pl.BlockSpec((None, s, LANES), lambda bi, hp, off=off: (bi, 0, off + hp))

    table = pl.BlockSpec((s, LANES), lambda bi, hp: (0, 0))
    return pl.pallas_call(
        _attention_kernel,
        grid=(b, pairs),
        in_specs=[cols(0), cols(pairs), cols(2 * pairs), table, table, table],
        out_specs=pl.BlockSpec((None, s, LANES), lambda bi, hp: (bi, 0, hp)),
        out_shape=jax.ShapeDtypeStruct((b, s, ATTN_WIDTH), jnp.float32),
        scratch_shapes=[pltpu.VMEM((s, LANES), jnp.float32)] * 2
        + [pltpu.VMEM((s, LANES), jnp.bfloat16)] * 4
        + [pltpu.VMEM((s, LANES), jnp.float32)] * 3,
        compiler_params=pltpu.CompilerParams(
            dimension_semantics=("arbitrary", "arbitrary"),
            vmem_limit_bytes=56 * 1024 * 1024),
        name="dilated_attention",
    )(proj, proj, proj, cos_t, sin_lo, sin_hi)


SSD_HALO = 8
SSD_GROUP_HEADS = SSD_HEADS // SSD_GROUPS
SSD_GROUP_WIDTH = SSD_GROUP_HEADS * SSD_HEAD_DIM


def _softplus(v):
    return jnp.maximum(v, 0.0) + jnp.log(1.0 + jnp.exp(-jnp.abs(v)))


def _expand_heads(mat, first_head):
    t = mat.shape[0]
    low = lax.broadcasted_iota(jnp.int32, (1, LANES), 1) < SSD_HEAD_DIM
    tiles = []
    for j in range(0, SSD_GROUP_HEADS, 2):
        a = jnp.broadcast_to(mat[:, first_head + j:first_head + j + 1], (t, LANES))
        b = jnp.broadcast_to(mat[:, first_head + j + 1:first_head + j + 2], (t, LANES))
        tiles.append(jnp.where(low, a, b))
    return jnp.concatenate(tiles, axis=-1)


def _ssd_kernel(xbc_ref, prev_ref, next_ref, dt_ref, cw_ref, cb_ref, dtb_ref, alog_ref, dsk_ref,
                y_ref, ext, cv, state, *, reverse):
    i = pl.program_id(1)
    nc = pl.num_programs(1)
    c = (nc - 1 - i) if reverse else i
    t = SSD_CHUNK
    bf16 = jnp.bfloat16

    @pl.when(i == 0)
    def _():
        state[...] = jnp.zeros_like(state)

    ext[0:SSD_HALO, :] = jnp.where(c > 0, prev_ref[...], 0.0)
    ext[SSD_HALO:SSD_HALO + t, :] = xbc_ref[...]
    ext[SSD_HALO + t:, :] = jnp.where(c < nc - 1, next_ref[...], 0.0)
    for j in range(SSD_CONV_CH // LANES):
        cols = slice(j * LANES, (j + 1) * LANES)
        acc = jnp.broadcast_to(cb_ref[:, cols], (t, LANES))
        for k in range(SSD_CONV):
            acc = acc + cw_ref[k:k + 1, cols] * ext[pl.ds(SSD_HALO - SSD_CONV // 2 + k, t), cols]
        cv[:, cols] = acc * jax.nn.sigmoid(acc)

    dt = _softplus(dt_ref[...] + dtb_ref[...])
    a = dt * (-jnp.exp(alog_ref[...]))
    row = lax.broadcasted_iota(jnp.int32, (t, LANES), 0)
    lane = lax.broadcasted_iota(jnp.int32, (t, LANES), 1)
    cs = a
    shift = 1
    while shift < t:
        cs = cs + jnp.where(row >= shift, pltpu.roll(cs, shift, 0), 0.0)
        shift *= 2
    total = cs[t - 1:t, :]
    e = jnp.where(lane < SSD_HEADS, cs, total - cs + a)
    off = SSD_HEADS if reverse else 0
    edge = e[0:1, :] if reverse else e[t - 1:t, :]
    scale_y = jnp.exp(e)
    scale_s = jnp.exp(edge - e) * dt

    x_off = 0
    b_off = SSD_WIDTH
    c_off = SSD_WIDTH + SSD_GROUPS * SSD_STATE
    low = lax.broadcasted_iota(jnp.int32, (1, LANES), 1) < SSD_HEAD_DIM

    if not reverse:
        e_t = e.T
        dt_t = dt.T
        li = lax.broadcasted_iota(jnp.int32, (t, t), 0)
        si = lax.broadcasted_iota(jnp.int32, (t, t), 1)
        lower = si <= li
        upper = si >= li

    for g in range(SSD_GROUPS):
        bg = cv[:, b_off + g * SSD_STATE:b_off + (g + 1) * SSD_STATE].astype(bf16)
        cg = cv[:, c_off + g * SSD_STATE:c_off + (g + 1) * SSD_STATE].astype(bf16)
        xg = cv[:, x_off + g * SSD_GROUP_WIDTH:x_off + (g + 1) * SSD_GROUP_WIDTH]
        h0 = g * SSD_GROUP_HEADS
        st = state[g]
        y = jnp.dot(cg, st.astype(bf16), preferred_element_type=jnp.float32) \
            * _expand_heads(scale_y, off + h0)
        xw = (xg * _expand_heads(scale_s, off + h0)).astype(bf16)
        new = lax.dot_general(bg, xw, (((0,), (0,)), ((), ())),
                              preferred_element_type=jnp.float32)
        decay = _expand_heads(jnp.broadcast_to(jnp.exp(edge), (SSD_HALO, LANES)), off + h0)
        state[g] = st * decay[0:1, :] + new

        if not reverse:
            cbm = lax.dot_general(cg, bg, (((1,), (1,)), ((), ())),
                                  preferred_element_type=jnp.float32)
            pieces = []
            for j in range(0, SSD_GROUP_HEADS, 2):
                xp = xg[:, j * SSD_HEAD_DIM:(j + 2) * SSD_HEAD_DIM]
                yp = None
                for hh, keep in ((j, low), (j + 1, jnp.logical_not(low))):
                    h = h0 + hh
                    hb = SSD_HEADS + h
                    lf = jnp.exp(jnp.where(lower, e[:, h:h + 1] - e_t[h:h + 1, :], -jnp.inf))
                    lb = jnp.exp(jnp.where(upper, e[:, hb:hb + 1] - e_t[hb:hb + 1, :], -jnp.inf))
                    mh = cbm * (lf * dt_t[h:h + 1, :] + lb * dt_t[hb:hb + 1, :])
                    part = jnp.dot(mh.astype(bf16), jnp.where(keep, xp, 0.0).astype(bf16),
                                   preferred_element_type=jnp.float32)
                    yp = part if yp is None else yp + part
                pieces.append(yp)
            gcols = slice(g * SSD_GROUP_WIDTH, (g + 1) * SSD_GROUP_WIDTH)
            y = y + jnp.concatenate(pieces, axis=-1) + dsk_ref[:, gcols] * xg
        y_ref[:, g * SSD_GROUP_WIDTH:(g + 1) * SSD_GROUP_WIDTH] = y


def _ssd_scan(proj, conv_w, conv_b, dt_bias, a_log, d_skip, reverse):
    b, s, _ = proj.shape
    nc = s // SSD_CHUNK
    z_end = 3 * ATTN_WIDTH + SSD_WIDTH
    xbc_blk = z_end // SSD_CONV_CH
    dt_blk = (z_end + SSD_CONV_CH) // LANES
    assert z_end % SSD_CONV_CH == 0 and (z_end + SSD_CONV_CH) % LANES == 0
    halo_per_chunk = SSD_CHUNK // SSD_HALO
    n_halo = s // SSD_HALO

    def chunk(i):
        return (nc - 1 - i) if reverse else i

    cw = jnp.pad(conv_w, ((0, SSD_HALO - SSD_CONV), (0, 0)))
    pad16 = lambda v: jnp.pad(v, (0, LANES - 2 * SSD_HEADS)).reshape(1, LANES)
    vec = lambda width: pl.BlockSpec((1, width), lambda bi, i: (0, 0))
    return pl.pallas_call(
        functools.partial(_ssd_kernel, reverse=reverse),
        grid=(b, nc),
        in_specs=[
            pl.BlockSpec((None, SSD_CHUNK, SSD_CONV_CH), lambda bi, i: (bi, chunk(i), xbc_blk)),
            pl.BlockSpec((None, SSD_HALO, SSD_CONV_CH),
                         lambda bi, i: (bi, jnp.maximum(chunk(i) * halo_per_chunk - 1, 0), xbc_blk)),
            pl.BlockSpec((None, SSD_HALO, SSD_CONV_CH),
                         lambda bi, i: (bi, jnp.minimum((chunk(i) + 1) * halo_per_chunk, n_halo - 1),
                                        xbc_blk)),
            pl.BlockSpec((None, SSD_CHUNK, LANES), lambda bi, i: (bi, chunk(i), dt_blk)),
            pl.BlockSpec((SSD_HALO, SSD_CONV_CH), lambda bi, i: (0, 0)),
            vec(SSD_CONV_CH), vec(LANES), vec(LANES), vec(SSD_WIDTH),
        ],
        out_specs=pl.BlockSpec((None, SSD_CHUNK, SSD_WIDTH), lambda bi, i: (bi, chunk(i), 0)),
        out_shape=jax.ShapeDtypeStruct((b, s, SSD_WIDTH), jnp.float32),
        scratch_shapes=[
            pltpu.VMEM((SSD_CHUNK + 2 * SSD_HALO, SSD_CONV_CH), jnp.float32),
            pltpu.VMEM((SSD_CHUNK, SSD_CONV_CH), jnp.float32),
            pltpu.VMEM((SSD_GROUPS, SSD_STATE, SSD_GROUP_WIDTH), jnp.float32),
        ],
        compiler_params=pltpu.CompilerParams(
            dimension_semantics=("arbitrary", "arbitrary"),
            vmem_limit_bytes=32 * 1024 * 1024),
        name="ssd_bwd" if reverse else "ssd_fwd",
    )(proj, proj, proj, proj, cw, conv_b.reshape(1, SSD_CONV_CH), pad16(dt_bias), pad16(a_log),
      jnp.repeat(d_skip, SSD_HEAD_DIM).reshape(1, SSD_WIDTH))


def _ssd_pre_gate(proj, conv_w, conv_b, dt_bias_fwd, a_log_fwd, dt_bias_bwd, a_log_bwd, d_skip):
    dt_bias = jnp.concatenate([dt_bias_fwd, dt_bias_bwd])
    a_log = jnp.concatenate([a_log_fwd, a_log_bwd])
    y_f = _ssd_scan(proj, conv_w, conv_b, dt_bias, a_log, d_skip, reverse=False)
    y_b = _ssd_scan(proj, conv_w, conv_b, dt_bias, a_log, d_skip, reverse=True)
    return y_f, y_b


def _mix_kernel(attn_ref, yf_ref, yb_ref, z_ref, x_ref, anw_ref, snw_ref, wout_ref,
                l1w_ref, l1b_ref, rw_hi_ref, rw_lo_ref, rb_ref, tri_ref,
                h_ref, e_ref, g_ref, r_ref, cnt_ref, carry):
    bf16 = jnp.bfloat16

    @pl.when(pl.program_id(0) == 0)
    def _():
        carry[...] = jnp.zeros_like(carry)

    a = attn_ref[...]
    a = a * lax.rsqrt(jnp.mean(a * a, axis=-1, keepdims=True) + NORM_EPS) * anw_ref[...]
    z = z_ref[...]
    y = (yf_ref[...] + yb_ref[...]) * (z * jax.nn.sigmoid(z))
    groups = []
    for g in range(SSD_GROUPS):
        yg = y[:, g * SSD_GROUP_WIDTH:(g + 1) * SSD_GROUP_WIDTH]
        groups.append(yg * lax.rsqrt(jnp.mean(yg * yg, axis=-1, keepdims=True) + NORM_EPS))
    y = jnp.concatenate(groups, axis=-1) * snw_ref[...]
    mix = (jnp.dot(a.astype(bf16), wout_ref[0:ATTN_WIDTH, :], preferred_element_type=jnp.float32)
           + jnp.dot(y.astype(bf16), wout_ref[ATTN_WIDTH:, :], preferred_element_type=jnp.float32))
    v = DEEPNORM_ALPHA * x_ref[...] + mix
    mu = jnp.mean(v, axis=-1, keepdims=True)
    var = jnp.mean(jnp.square(v - mu), axis=-1, keepdims=True)
    h1 = (v - mu) * lax.rsqrt(var + NORM_EPS) * l1w_ref[...] + l1b_ref[...]
    for c in range(SLAB):
        h_ref[pl.ds(c, MIX_ROWS, stride=SLAB), :] = h1[:, c * LANES:(c + 1) * LANES]

    h_hi = h1.astype(bf16)
    h_lo = (h1 - h_hi.astype(jnp.float32)).astype(bf16)
    logits = (jnp.dot(h_hi, rw_hi_ref[...], preferred_element_type=jnp.float32)
              + jnp.dot(h_lo, rw_hi_ref[...], preferred_element_type=jnp.float32)
              + jnp.dot(h_hi, rw_lo_ref[...], preferred_element_type=jnp.float32)
              + rb_ref[...])
    lane = lax.broadcasted_iota(jnp.int32, (MIX_ROWS, LANES), 1)
    lane_f = lane.astype(jnp.float32)
    cur = jnp.where(lane < N_EXPERTS, logits, -jnp.inf)
    vals, idxs = [], []
    for _ in range(TOP_K):
        m = jnp.max(cur, axis=-1, keepdims=True)
        idx = jnp.min(jnp.where(cur == m, lane_f, float(LANES)), axis=-1, keepdims=True)
        vals.append(m)
        idxs.append(idx)
        cur = jnp.where(lane_f == idx, -jnp.inf, cur)
    probs = [jnp.exp(val - vals[0]) for val in vals]
    den = probs[0]
    for p in probs[1:]:
        den = den + p

    sel = jnp.zeros((MIX_ROWS, LANES), jnp.float32)
    for k in range(TOP_K):
        sel = sel + jnp.where(lane_f == idxs[k], 1.0, 0.0)
    rank_all = jnp.dot(tri_ref[...], sel.astype(bf16), preferred_element_type=jnp.float32) \
        + carry[...]
    e_out = jnp.zeros((MIX_ROWS, LANES), jnp.float32)
    g_out = jnp.zeros((MIX_ROWS, LANES), jnp.float32)
    r_out = jnp.zeros((MIX_ROWS, LANES), jnp.float32)
    for k in range(TOP_K):
        rank_k = jnp.sum(jnp.where(lane_f == idxs[k], rank_all, 0.0), axis=-1, keepdims=True)
        e_out = jnp.where(lane == k, idxs[k], e_out)
        g_out = jnp.where(lane == k, probs[k] / den, g_out)
        r_out = jnp.where(lane == k, rank_k, r_out)
    e_ref[...] = e_out.astype(jnp.int32)
    g_ref[...] = g_out
    r_ref[...] = r_out.astype(jnp.int32)
    carry[...] = carry[...] + jnp.sum(sel, axis=0, keepdims=True)
    cnt_ref[...] = jnp.broadcast_to(carry[...], cnt_ref.shape).astype(jnp.int32)


def _mix_ln_router(attn, y_f, y_b, proj, x, attn_norm_w, ssd_norm_w, w_out, ln1_w, ln1_b,
                   router_w, router_b):
    n, d = x.shape
    z_blk = (3 * ATTN_WIDTH) // SSD_WIDTH
    rw = jnp.pad(router_w, ((0, 0), (0, LANES - N_EXPERTS)))
    rw_hi = rw.astype(jnp.bfloat16)
    rw_lo = (rw - rw_hi.astype(jnp.float32)).astype(jnp.bfloat16)
    rb = jnp.pad(router_b, (0, LANES - N_EXPERTS)).reshape(1, LANES)
    tri = (jnp.arange(MIX_ROWS)[None, :] < jnp.arange(MIX_ROWS)[:, None]).astype(jnp.bfloat16)
    half = pl.BlockSpec((MIX_ROWS, ATTN_WIDTH), lambda i: (i, 0))
    vec = lambda width: pl.BlockSpec((1, width), lambda i: (0, 0))
    full = lambda r, c: pl.BlockSpec((r, c), lambda i: (0, 0))
    rows128 = pl.BlockSpec((MIX_ROWS, LANES), lambda i: (i, 0))
    return pl.pallas_call(
        _mix_kernel,
        grid=(n // MIX_ROWS,),
        in_specs=[half, half, half,
                  pl.BlockSpec((MIX_ROWS, SSD_WIDTH), lambda i: (i, z_blk)),
                  pl.BlockSpec((MIX_ROWS, d), lambda i: (i, 0)),
                  vec(ATTN_WIDTH), vec(SSD_WIDTH), full(d, d), vec(d), vec(d),
                  full(d, LANES), full(d, LANES), vec(LANES), full(MIX_ROWS, MIX_ROWS)],
        out_specs=[pl.BlockSpec((MIX_ROWS * SLAB, LANES), lambda i: (i, 0)),
                   rows128, rows128, rows128,
                   pl.BlockSpec((SLAB, LANES), lambda i: (0, 0))],
        out_shape=[jax.ShapeDtypeStruct((n * SLAB, LANES), jnp.float32),
                   jax.ShapeDtypeStruct((n, LANES), jnp.int32),
                   jax.ShapeDtypeStruct((n, LANES), jnp.float32),
                   jax.ShapeDtypeStruct((n, LANES), jnp.int32),
                   jax.ShapeDtypeStruct((SLAB, LANES), jnp.int32)],
        scratch_shapes=[pltpu.VMEM((1, LANES), jnp.float32)],
        compiler_params=pltpu.CompilerParams(
            dimension_semantics=("arbitrary",),
            vmem_limit_bytes=48 * 1024 * 1024),
        name="mix_ln_router",
    )(attn, y_f, y_b, proj, x, attn_norm_w.reshape(1, ATTN_WIDTH), ssd_norm_w.reshape(1, SSD_WIDTH),
      w_out.astype(jnp.bfloat16), ln1_w.reshape(1, d), ln1_b.reshape(1, d), rw_hi, rw_lo, rb, tri)


def _slab_of(row):
    return pl.ds(pl.multiple_of(row * SLAB, SLAB), SLAB)


def _dispatch_kernel(last_ref, has_ref, nt_ref, pos_ref, h_ref, xs_hbm, zbuf, sem, zsem):
    i = pl.program_id(0)
    n_copies = TOP_K * MIX_ROWS
    t_max = xs_hbm.shape[0] // (MOE_TILE * SLAB)

    @pl.when(i == 0)
    def _zero_fill():
        zbuf[...] = jnp.zeros_like(zbuf)

        def fill(tile):
            return pltpu.make_async_copy(
                zbuf, xs_hbm.at[pl.ds(pl.multiple_of(tile * (MOE_TILE * SLAB), MOE_TILE * SLAB),
                                      MOE_TILE * SLAB), :], zsem)

        for e in range(N_EXPERTS):
            @pl.when(has_ref[e] > 0)
            def _(e=e):
                fill(last_ref[e]).start()
        for j in range(N_EXPERTS + 1):
            @pl.when(nt_ref[0] + j < t_max)
            def _(j=j):
                fill(nt_ref[0] + j).start()
        for e in range(N_EXPERTS):
            @pl.when(has_ref[e] > 0)
            def _(e=e):
                fill(last_ref[e]).wait()
        for j in range(N_EXPERTS + 1):
            @pl.when(nt_ref[0] + j < t_max)
            def _(j=j):
                fill(nt_ref[0] + j).wait()

    def copy(j):
        return pltpu.make_async_copy(h_ref.at[_slab_of(j % MIX_ROWS), :],
                                     xs_hbm.at[_slab_of(pos_ref[0, 0, j]), :], sem)

    def start(j, c):
        copy(j).start()
        return c

    def wait(j, c):
        copy(j).wait()
        return c

    lax.fori_loop(0, n_copies, start, 0, unroll=8)
    lax.fori_loop(0, n_copies, wait, 0, unroll=8)


def _dispatch(h1_slab, pos_tab, last_tile, has_tile, n_tiles, n_rows):
    nb = pos_tab.shape[0]
    grid_spec = pltpu.PrefetchScalarGridSpec(
        num_scalar_prefetch=3,
        grid=(nb,),
        in_specs=[pl.BlockSpec((1, 1, TOP_K * MIX_ROWS), lambda i, la, ha, nt: (i, 0, 0),
                               memory_space=pltpu.SMEM),
                  pl.BlockSpec((MIX_ROWS * SLAB, LANES), lambda i, la, ha, nt: (i, 0))],
        out_specs=pl.BlockSpec(memory_space=pl.ANY),
        scratch_shapes=[pltpu.VMEM((MOE_TILE * SLAB, LANES), jnp.float32),
                        pltpu.SemaphoreType.DMA(()),
                        pltpu.SemaphoreType.DMA(())],
    )
    return pl.pallas_call(
        _dispatch_kernel,
        grid_spec=grid_spec,
        out_shape=jax.ShapeDtypeStruct((n_rows * SLAB, LANES), jnp.float32),
        compiler_params=pltpu.CompilerParams(
            dimension_semantics=("arbitrary",),
            vmem_limit_bytes=32 * 1024 * 1024),
        name="dispatch",
    )(last_tile, has_tile, n_tiles, pos_tab, h1_slab)


def _moe_kernel(te_ref, nt_ref, dst_prv_ref,
                x_ref, wg_ref, wu_ref, wd_ref, bg_ref, bu_ref, bd_ref,
                y_hbm,
                ybuf, xb_s, act_s, wg_bf, wu_bf, wd_bf, ssem):
    i = pl.program_id(0)
    n_tiles = nt_ref[0]
    slot = i % 2
    other = 1 - slot
    n_pairs = y_hbm.shape[0] // SLAB - 2 * MOE_TILE
    bf16 = jnp.bfloat16

    def scatter_copy(dst, row, s):
        return pltpu.make_async_copy(ybuf.at[s, _slab_of(row), :],
                                     y_hbm.at[_slab_of(dst), :], ssem.at[s])

    def start_scatter_prev():
        def body(r, c):
            dst = jnp.where(i == 0, n_pairs + MOE_TILE + r, dst_prv_ref[0, 0, r])
            scatter_copy(dst, r, other).start()
            return c
        lax.fori_loop(0, MOE_TILE, body, 0, unroll=8)

    def wait_scatter(s):
        def body(r, c):
            scatter_copy(0, r, s).wait()
            return c
        lax.fori_loop(0, MOE_TILE, body, 0, unroll=8)

    @pl.when(i == 0)
    def _first():
        ybuf[...] = jnp.zeros_like(ybuf)
        for half in range(2):
            cp = pltpu.make_async_copy(
                ybuf.at[0],
                y_hbm.at[pl.ds((n_pairs + half * MOE_TILE) * SLAB, MOE_TILE * SLAB), :],
                ssem.at[half])
            cp.start()
            cp.wait()

    @pl.when((i >= 1) & (i <= n_tiles))
    def _arrive():
        wait_scatter(slot)

    @pl.when(i <= n_tiles)
    def _scatter_prev():
        start_scatter_prev()

    @pl.when(i < n_tiles)
    def _compute():
        e = te_ref[i]
        prev_e = te_ref[jnp.maximum(i - 1, 0)]

        @pl.when((i == 0) | (e != prev_e))
        def _():
            wg_bf[...] = wg_ref[0].astype(bf16)
            wu_bf[...] = wu_ref[0].astype(bf16)
            wd_bf[...] = wd_ref[0].astype(bf16)

        for c in range(SLAB):
            xb_s[:, c * LANES:(c + 1) * LANES] = \
                x_ref[pl.ds(c, MOE_TILE, stride=SLAB), :].astype(bf16)

        for c in range(D_MODEL // FFN_CHUNK):
            cols = slice(c * FFN_CHUNK, (c + 1) * FFN_CHUNK)
            g = jnp.dot(xb_s[...], wg_bf[:, cols], preferred_element_type=jnp.float32)
            g = g + bg_ref[pl.ds(e, 1), cols]
            u = jnp.dot(xb_s[...], wu_bf[:, cols], preferred_element_type=jnp.float32)
            u = u + bu_ref[pl.ds(e, 1), cols]
            g = jnp.minimum(g, SWIGLU_LIMIT)
            u = jnp.clip(u, -SWIGLU_LIMIT, SWIGLU_LIMIT)
            act_s[:, cols] = ((u + 1.0) * g * jax.nn.sigmoid(SWIGLU_ALPHA * g)).astype(bf16)

        for c in range(D_MODEL // FFN_CHUNK):
            cols = slice(c * FFN_CHUNK, (c + 1) * FFN_CHUNK)
            y = jnp.dot(act_s[...], wd_bf[:, cols], preferred_element_type=jnp.float32)
            y = y + bd_ref[pl.ds(e, 1), cols]
            for j in range(FFN_CHUNK // LANES):
                ybuf[slot, pl.ds(c * (FFN_CHUNK // LANES) + j, MOE_TILE, stride=SLAB), :] = \
                    y[:, j * LANES:(j + 1) * LANES]

    @pl.when(i == n_tiles)
    def _drain():
        wait_scatter(other)


def _moe_ffn(xs, tile_e, n_tiles, dst_row, w_gate, b_gate, w_up, b_up, w_down, b_down,
             n_out_rows):
    t_max = tile_e.shape[0]
    d = D_MODEL
    w_spec = pl.BlockSpec((1, d, d), lambda i, te, nt: (te[i], 0, 0))
    b_spec = pl.BlockSpec((N_EXPERTS, d), lambda i, te, nt: (0, 0))
    grid_spec = pltpu.PrefetchScalarGridSpec(
        num_scalar_prefetch=2,
        grid=(t_max,),
        in_specs=[pl.BlockSpec((1, 1, MOE_TILE), lambda i, te, nt: (jnp.maximum(i - 1, 0), 0, 0),
                               memory_space=pltpu.SMEM),
                  pl.BlockSpec((MOE_TILE * SLAB, LANES),
                               lambda i, te, nt: (jnp.minimum(i, nt[0] - 1), 0)),
                  w_spec, w_spec, w_spec, b_spec, b_spec, b_spec],
        out_specs=pl.BlockSpec(memory_space=pl.ANY),
        scratch_shapes=[
            pltpu.VMEM((2, MOE_TILE * SLAB, LANES), jnp.float32),
            pltpu.VMEM((MOE_TILE, d), jnp.bfloat16),
            pltpu.VMEM((MOE_TILE, d), jnp.bfloat16),
            pltpu.VMEM((d, d), jnp.bfloat16),
            pltpu.VMEM((d, d), jnp.bfloat16),
            pltpu.VMEM((d, d), jnp.bfloat16),
            pltpu.SemaphoreType.DMA((2,)),
        ],
    )
    return pl.pallas_call(
        _moe_kernel,
        grid_spec=grid_spec,
        out_shape=jax.ShapeDtypeStruct((n_out_rows * SLAB, LANES), jnp.float32),
        compiler_params=pltpu.CompilerParams(
            dimension_semantics=("arbitrary",),
            vmem_limit_bytes=V7X_VMEM_BYTES - 6 * 1024 * 1024),
        name="moe_ffn",
    )(tile_e, n_tiles, dst_row, xs, w_gate, w_up, w_down, b_gate, b_up, b_down)


def _combine_ln_kernel(y0_ref, y1_ref, y2_ref, y3_ref, h_ref, g_ref, w_ref, b_ref, o_ref):
    g = g_ref[...]
    ffn = (g[:, 0:1] * _slab_rows(y0_ref, LN_ROWS) + g[:, 1:2] * _slab_rows(y1_ref, LN_ROWS)
           + g[:, 2:3] * _slab_rows(y2_ref, LN_ROWS) + g[:, 3:4] * _slab_rows(y3_ref, LN_ROWS))
    v = DEEPNORM_ALPHA * _slab_rows(h_ref, LN_ROWS) + ffn
    mu = jnp.mean(v, axis=-1, keepdims=True)
    var = jnp.mean(jnp.square(v - mu), axis=-1, keepdims=True)
    o_ref[...] = (v - mu) * lax.rsqrt(var + NORM_EPS) * w_ref[...] + b_ref[...]


def _combine_ln(y_planes, h1_slab, gates, ln_w, ln_b):
    n = gates.shape[0]
    d = D_MODEL
    nb = n // LN_ROWS

    def plane(k):
        return pl.BlockSpec((LN_ROWS * SLAB, LANES), lambda i, k=k: (k * nb + i, 0))

    vec = pl.BlockSpec((1, d), lambda i: (0, 0))
    return pl.pallas_call(
        _combine_ln_kernel,
        grid=(nb,),
        in_specs=[plane(0), plane(1), plane(2), plane(3), plane(0),
                  pl.BlockSpec((LN_ROWS, LANES), lambda i: (i, 0)), vec, vec],
        out_specs=pl.BlockSpec((LN_ROWS, d), lambda i: (i, 0)),
        out_shape=jax.ShapeDtypeStruct((n, d), jnp.float32),
        compiler_params=pltpu.CompilerParams(
            dimension_semantics=("arbitrary",),
            vmem_limit_bytes=48 * 1024 * 1024),
        name="combine_ln",
    )(y_planes, y_planes, y_planes, y_planes, h1_slab, gates,
      ln_w.reshape(1, d), ln_b.reshape(1, d))


def _routing_tables(top_e, rank, counts, n_tok):
    n_pairs = TOP_K * n_tok
    t_max = n_pairs // MOE_TILE + N_EXPERTS + 1
    n_rows = t_max * MOE_TILE
    tiles_per_e = (counts + MOE_TILE - 1) // MOE_TILE
    tile_end = jnp.cumsum(tiles_per_e)
    tile_start = tile_end - tiles_per_e
    n_tiles = tile_end[-1]
    tile_ids = jnp.arange(t_max, dtype=jnp.int32)
    tile_e = jnp.searchsorted(tile_end, jnp.minimum(tile_ids, n_tiles - 1), side="right")
    tile_e = jnp.minimum(tile_e, N_EXPERTS - 1).astype(jnp.int32)

    base = (tile_start * MOE_TILE).astype(jnp.int32)
    experts = jnp.arange(N_EXPERTS, dtype=jnp.int32)
    pos = jnp.sum(jnp.where(top_e[:, :, None] == experts[None, None, :], base[None, None, :], 0),
                  axis=-1) + rank
    pair_id = (jnp.arange(n_tok, dtype=jnp.int32)[:, None]
               + n_tok * jnp.arange(TOP_K, dtype=jnp.int32)[None, :])
    owner = jnp.full((n_rows,), -1, jnp.int32).at[pos.reshape(-1)].set(
        pair_id.reshape(-1), unique_indices=True)
    row_tile = jnp.arange(n_rows, dtype=jnp.int32) // MOE_TILE
    spare = n_pairs + (row_tile % 2) * MOE_TILE + jnp.arange(n_rows, dtype=jnp.int32) % MOE_TILE
    dst_row = jnp.where(owner >= 0, owner, spare).reshape(t_max, 1, MOE_TILE)

    nb = n_tok // MIX_ROWS
    pos_tab = pos.reshape(nb, MIX_ROWS, TOP_K).transpose(0, 2, 1).reshape(nb, 1, TOP_K * MIX_ROWS)
    last_tile = jnp.maximum(tile_end - 1, 0).astype(jnp.int32)
    has_tile = (tiles_per_e > 0).astype(jnp.int32)
    return (tile_e, n_tiles.reshape(1).astype(jnp.int32), pos_tab, dst_row, last_tile, has_tile,
            n_rows)


def kernel(x, w_in, attn_norm_w, conv_w, conv_b, dt_bias_fwd, a_log_fwd, dt_bias_bwd,
           a_log_bwd, d_skip, ssd_norm_w, w_out, ln1_w, ln1_b, router_w, router_b,
           w_gate, b_gate, w_up, b_up, w_down, b_down, ln2_w, ln2_b):
    b, s, d = x.shape
    n = b * s
    xf = x.reshape(n, d)
    in_cols = w_in.shape[-1]
    in_cols_pad = -(-in_cols // LANES) * LANES

    w_in_b = jnp.pad(w_in[0], ((0, 0), (0, in_cols_pad - in_cols))).astype(jnp.bfloat16)
    proj = _rows_matmul(xf, w_in_b, "in_proj")
    proj3 = proj.reshape(b, s, in_cols_pad)
    attn = _attention(proj3)
    y_f, y_b = _ssd_pre_gate(proj3, conv_w[0], conv_b[0], dt_bias_fwd[0], a_log_fwd[0],
                             dt_bias_bwd[0], a_log_bwd[0], d_skip[0])
    h1_slab, top_e, gates, rank, counts = _mix_ln_router(
        attn.reshape(n, ATTN_WIDTH), y_f.reshape(n, SSD_WIDTH), y_b.reshape(n, SSD_WIDTH),
        proj, xf, attn_norm_w[0], ssd_norm_w[0], w_out[0], ln1_w[0], ln1_b[0],
        router_w[0], router_b[0])

    tile_e, n_tiles, pos_tab, dst_row, last_tile, has_tile, n_rows = _routing_tables(
        top_e[:, :TOP_K], rank[:, :TOP_K], counts[0, :N_EXPERTS], n)
    xs = _dispatch(h1_slab, pos_tab, last_tile, has_tile, n_tiles, n_rows)
    n_out_rows = TOP_K * n + 2 * MOE_TILE
    y_planes = _moe_ffn(xs, tile_e, n_tiles, dst_row,
                        w_gate[0], b_gate[0], w_up[0], b_up[0], w_down[0], b_down[0],
                        n_out_rows)
    out = _combine_ln(y_planes, h1_slab, gates, ln2_w[0], ln2_b[0])
    return out.reshape(b, s, d)
```

```python
import functools

import jax
import jax.numpy as jnp
from jax import lax
from jax.experimental import pallas as pl
from jax.experimental.pallas import tpu as pltpu

D_MODEL = 1024
HEAD_DIM = 64
ATTN_HEADS = 8
ATTN_WIDTH = ATTN_HEADS * HEAD_DIM
DILATED_PATTERNS = ((128, 1), (512, 4), (2048, 16))
ROPE_THETA = 10000.0
SSD_HEADS = 8
SSD_HEAD_DIM = 64
SSD_WIDTH = SSD_HEADS * SSD_HEAD_DIM
SSD_GROUPS = 2
SSD_STATE = 128
SSD_CONV = 5
SSD_CHUNK = 128
SSD_CONV_CH = SSD_WIDTH + 2 * SSD_GROUPS * SSD_STATE
N_EXPERTS = 32
TOP_K = 4
SWIGLU_ALPHA = 1.702
SWIGLU_LIMIT = 7.0
DEPTH = 1
DEEPNORM_ALPHA = (2.0 * DEPTH) ** 0.25
NORM_EPS = 1e-5

V7X_VMEM_BYTES = 64 * 1024 * 1024
LANES = 128

MATMUL_ROWS = 512
MIX_ROWS = 512
MOE_TILE = 512
FFN_CHUNK = 256
LN_ROWS = 512
SLAB = D_MODEL // LANES


def _slab_rows(ref, rows):
    return jnp.concatenate(
        [ref[pl.ds(c, rows, stride=SLAB), :] for c in range(SLAB)], axis=-1)


def _rows_matmul_kernel(x_ref, w_ref, o_ref):
    o_ref[...] = jnp.dot(x_ref[...].astype(jnp.bfloat16), w_ref[...],
                         preferred_element_type=jnp.float32)


def _rows_matmul(x, w_bf16, name):
    n, k = x.shape
    _, m = w_bf16.shape
    assert n % MATMUL_ROWS == 0 and m % LANES == 0
    return pl.pallas_call(
        _rows_matmul_kernel,
        grid=(n // MATMUL_ROWS,),
        in_specs=[pl.BlockSpec((MATMUL_ROWS, k), lambda i: (i, 0)),
                  pl.BlockSpec((k, m), lambda i: (0, 0))],
        out_specs=pl.BlockSpec((MATMUL_ROWS, m), lambda i: (i, 0)),
        out_shape=jax.ShapeDtypeStruct((n, m), jnp.float32),
        compiler_params=pltpu.CompilerParams(
            dimension_semantics=("arbitrary",),
            vmem_limit_bytes=48 * 1024 * 1024),
        name=name,
    )(x, w_bf16)


ATTN_BLOCK = 128
ATTN_WINDOW = 256
ATTN_HALF = 64
ATTN_STAGE_ROWS = 256


def _rope_tables(s):
    pos = jnp.arange(s, dtype=jnp.float32)
    inv_freq = ROPE_THETA ** (-jnp.arange(0, HEAD_DIM, 2, dtype=jnp.float32) / HEAD_DIM)
    ang = pos[:, None] * inv_freq[None, :]
    return jnp.cos(ang), jnp.sin(ang)


def _strided_rows(start, size, stride):
    return pl.ds(start, size) if stride == 1 else pl.ds(start, size, stride=stride)


def _attention_kernel(q_ref, k_ref, v_ref, cos_ref, sin_lo_ref, sin_hi_ref, o_ref,
                      qr, kr, q0s, q1s, ks, vs, acc_o, acc_m, acc_l):
    s_len = q_ref.shape[0]
    head0 = lax.broadcasted_iota(jnp.int32, (1, LANES), 1) < HEAD_DIM
    q_i = lax.broadcasted_iota(jnp.int32, (ATTN_BLOCK, ATTN_WINDOW), 0)
    k_j = lax.broadcasted_iota(jnp.int32, (ATTN_BLOCK, ATTN_WINDOW), 1)
    rel = q_i - k_j
    last = len(DILATED_PATTERNS) - 1

    def rotate(j, carry):
        rows = pl.ds(pl.multiple_of(j * ATTN_STAGE_ROWS, ATTN_STAGE_ROWS), ATTN_STAGE_ROWS)
        cos = cos_ref[rows, :]
        sin_lo = sin_lo_ref[rows, :]
        sin_hi = sin_hi_ref[rows, :]

        def rope(t):
            return (t * cos + pltpu.roll(t, LANES - HEAD_DIM // 2, 1) * sin_lo
                    + pltpu.roll(t, HEAD_DIM // 2, 1) * sin_hi)

        qr[rows, :] = rope(q_ref[rows, :]) * (HEAD_DIM ** -0.5)
        kr[rows, :] = rope(k_ref[rows, :])
        return carry

    lax.fori_loop(0, s_len // ATTN_STAGE_ROWS, rotate, 0, unroll=2)

    for pi, (window, dil) in enumerate(DILATED_PATTERNS):
        assert window // (2 * dil) == ATTN_HALF
        row_len = s_len // dil
        assert row_len % ATTN_WINDOW == 0 and row_len % ATTN_STAGE_ROWS == 0

        def stage(j, carry, dil=dil, row_len=row_len):
            f0 = pl.multiple_of(j * ATTN_STAGE_ROWS, ATTN_STAGE_ROWS)
            r = f0 // row_len
            src = _strided_rows((f0 - r * row_len) * dil + r, ATTN_STAGE_ROWS, dil)
            dst = pl.ds(f0, ATTN_STAGE_ROWS)
            qv = qr[src, :]
            q0s[dst, :] = jnp.where(head0, qv, 0.0).astype(jnp.bfloat16)
            q1s[dst, :] = jnp.where(head0, 0.0, qv).astype(jnp.bfloat16)
            ks[dst, :] = kr[src, :].astype(jnp.bfloat16)
            vs[dst, :] = v_ref[src, :].astype(jnp.bfloat16)
            return carry

        lax.fori_loop(0, s_len // ATTN_STAGE_ROWS, stage, 0, unroll=2)

        def block(bi, carry, pi=pi, dil=dil, row_len=row_len):
            f0 = pl.multiple_of(bi * ATTN_BLOCK, ATTN_BLOCK)
            r = f0 // row_len
            row_start = r * row_len
            win = jnp.clip(f0 - ATTN_HALF, row_start, row_start + row_len - ATTN_WINDOW)
            win = pl.multiple_of(win, ATTN_HALF)
            valid = jnp.abs(rel + (f0 - win)) <= ATTN_HALF
            kw = ks[pl.ds(win, ATTN_WINDOW), :]
            vw = vs[pl.ds(win, ATTN_WINDOW), :]
            parts = []
            for qs_ref in (q0s, q1s):
                qb = qs_ref[pl.ds(f0, ATTN_BLOCK), :]
                sc = lax.dot_general(qb, kw, (((1,), (1,)), ((), ())),
                                     preferred_element_type=jnp.float32)
                sc = jnp.where(valid, sc, -jnp.inf)
                m = jnp.max(sc, axis=-1, keepdims=True)
                p = jnp.exp(sc - m)
                den = jnp.sum(p, axis=-1, keepdims=True)
                num = jnp.dot(p.astype(jnp.bfloat16), vw, preferred_element_type=jnp.float32)
                parts.append((num, m, den))
            num = jnp.where(head0, parts[0][0], parts[1][0])
            m = jnp.where(head0, parts[0][1], parts[1][1])
            den = jnp.where(head0, parts[0][2], parts[1][2])
            rows = _strided_rows((f0 - row_start) * dil + r, ATTN_BLOCK, dil)
            if pi == 0:
                acc_o[rows, :] = num
                acc_m[rows, :] = m
                acc_l[rows, :] = den
            else:
                m_old = acc_m[rows, :]
                m_new = jnp.maximum(m_old, m)
                w_old = jnp.exp(m_old - m_new)
                w_cur = jnp.exp(m - m_new)
                num = acc_o[rows, :] * w_old + num * w_cur
                den = acc_l[rows, :] * w_old + den * w_cur
                if pi == last:
                    o_ref[rows, :] = num / den
                else:
                    acc_o[rows, :] = num
                    acc_m[rows, :] = m_new
                    acc_l[rows, :] = den
            return carry

        lax.fori_loop(0, s_len // ATTN_BLOCK, block, 0, unroll=4)


def _attention(proj):
    b, s, _ = proj.shape
    cos, sin = _rope_tables(s)
    reps = LANES // (HEAD_DIM // 2)
    first_half = (jnp.arange(LANES) % HEAD_DIM) < HEAD_DIM // 2
    cos_t = jnp.tile(cos, (1, reps))
    sin_t = jnp.tile(sin, (1, reps))
    sin_lo = jnp.where(first_half, -sin_t, 0.0)
    sin_hi = jnp.where(first_half, 0.0, sin_t)
    pairs = ATTN_WIDTH // LANES

    def cols(off):
        return pl.BlockSpec((None, s, LANES), lambda bi, hp, off=off: (bi, 0, off + hp))

    table = pl.BlockSpec((s, LANES), lambda bi, hp: (0, 0))
    return pl.pallas_call(
        _attention_kernel,
        grid=(b, pairs),
        in_specs=[cols(0), cols(pairs), cols(2 * pairs), table, table, table],
        out_specs=pl.BlockSpec((None, s, LANES), lambda bi, hp: (bi, 0, hp)),
        out_shape=jax.ShapeDtypeStruct((b, s, ATTN_WIDTH), jnp.float32),
        scratch_shapes=[pltpu.VMEM((s, LANES), jnp.float32)] * 2
        + [pltpu.VMEM((s, LANES), jnp.bfloat16)] * 4
        + [pltpu.VMEM((s, LANES), jnp.float32)] * 3,
        compiler_params=pltpu.CompilerParams(
            dimension_semantics=("arbitrary", "arbitrary"),
            vmem_limit_bytes=56 * 1024 * 1024),
        name="dilated_attention",
    )(proj, proj, proj, cos_t, sin_lo, sin_hi)


SSD_HALO = 8
SSD_GROUP_HEADS = SSD_HEADS // SSD_GROUPS
SSD_GROUP_WIDTH = SSD_GROUP_HEADS * SSD_HEAD_DIM


def _softplus(v):
    return jnp.maximum(v, 0.0) + jnp.log(1.0 + jnp.exp(-jnp.abs(v)))


def _expand_heads(mat, first_head):
    t = mat.shape[0]
    low = lax.broadcasted_iota(jnp.int32, (1, LANES), 1) < SSD_HEAD_DIM
    tiles = []
    for j in range(0, SSD_GROUP_HEADS, 2):
        a = jnp.broadcast_to(mat[:, first_head + j:first_head + j + 1], (t, LANES))
        b = jnp.broadcast_to(mat[:, first_head + j + 1:first_head + j + 2], (t, LANES))
        tiles.append(jnp.where(low, a, b))
    return jnp.concatenate(tiles, axis=-1)


def _ssd_kernel(xbc_ref, prev_ref, next_ref, dt_ref, cw_ref, cb_ref, dtb_ref, alog_ref, dsk_ref,
                y_ref, ext, cv, state, *, reverse):
    i = pl.program_id(1)
    nc = pl.num_programs(1)
    c = (nc - 1 - i) if reverse else i
    t = SSD_CHUNK
    bf16 = jnp.bfloat16

    @pl.when(i == 0)
    def _():
        state[...] = jnp.zeros_like(state)

    ext[0:SSD_HALO, :] = jnp.where(c > 0, prev_ref[...], 0.0)
    ext[SSD_HALO:SSD_HALO + t, :] = xbc_ref[...]
    ext[SSD_HALO + t:, :] = jnp.where(c < nc - 1, next_ref[...], 0.0)
    for j in range(SSD_CONV_CH // LANES):
        cols = slice(j * LANES, (j + 1) * LANES)
        acc = jnp.broadcast_to(cb_ref[:, cols], (t, LANES))
        for k in range(SSD_CONV):
            acc = acc + cw_ref[k:k + 1, cols] * ext[pl.ds(SSD_HALO - SSD_CONV // 2 + k, t), cols]
        cv[:, cols] = acc * jax.nn.sigmoid(acc)

    dt = _softplus(dt_ref[...] + dtb_ref[...])
    a = dt * (-jnp.exp(alog_ref[...]))
    row = lax.broadcasted_iota(jnp.int32, (t, LANES), 0)
    lane = lax.broadcasted_iota(jnp.int32, (t, LANES), 1)
    cs = a
    shift = 1
    while shift < t:
        cs = cs + jnp.where(row >= shift, pltpu.roll(cs, shift, 0), 0.0)
        shift *= 2
    total = cs[t - 1:t, :]
    e = jnp.where(lane < SSD_HEADS, cs, total - cs + a)
    off = SSD_HEADS if reverse else 0
    edge = e[0:1, :] if reverse else e[t - 1:t, :]
    scale_y = jnp.exp(e)
    scale_s = jnp.exp(edge - e) * dt

    x_off = 0
    b_off = SSD_WIDTH
    c_off = SSD_WIDTH + SSD_GROUPS * SSD_STATE
    low = lax.broadcasted_iota(jnp.int32, (1, LANES), 1) < SSD_HEAD_DIM

    if not reverse:
        e_t = e.T
        dt_t = dt.T
        li = lax.broadcasted_iota(jnp.int32, (t, t), 0)
        si = lax.broadcasted_iota(jnp.int32, (t, t), 1)
        lower = si <= li
        upper = si >= li

    for g in range(SSD_GROUPS):
        bg = cv[:, b_off + g * SSD_STATE:b_off + (g + 1) * SSD_STATE].astype(bf16)
        cg = cv[:, c_off + g * SSD_STATE:c_off + (g + 1) * SSD_STATE].astype(bf16)
        xg = cv[:, x_off + g * SSD_GROUP_WIDTH:x_off + (g + 1) * SSD_GROUP_WIDTH]
        h0 = g * SSD_GROUP_HEADS
        st = state[g]
        y = jnp.dot(cg, st.astype(bf16), preferred_element_type=jnp.float32) \
            * _expand_heads(scale_y, off + h0)
        xw = (xg * _expand_heads(scale_s, off + h0)).astype(bf16)
        new = lax.dot_general(bg, xw, (((0,), (0,)), ((), ())),
                              preferred_element_type=jnp.float32)
        decay = _expand_heads(jnp.broadcast_to(jnp.exp(edge), (SSD_HALO, LANES)), off + h0)
        state[g] = st * decay[0:1, :] + new

        if not reverse:
            cbm = lax.dot_general(cg, bg, (((1,), (1,)), ((), ())),
                                  preferred_element_type=jnp.float32)
            pieces = []
            for j in range(0, SSD_GROUP_HEADS, 2):
                xp = xg[:, j * SSD_HEAD_DIM:(j + 2) * SSD_HEAD_DIM]
                yp = None
                for hh, keep in ((j, low), (j + 1, jnp.logical_not(low))):
                    h = h0 + hh
                    hb = SSD_HEADS + h
                    lf = jnp.exp(jnp.where(lower, e[:, h:h + 1] - e_t[h:h + 1, :], -jnp.inf))
                    lb = jnp.exp(jnp.where(upper, e[:, hb:hb + 1] - e_t[hb:hb + 1, :], -jnp.inf))
                    mh = cbm * (lf * dt_t[h:h + 1, :] + lb * dt_t[hb:hb + 1, :])
                    part = jnp.dot(mh.astype(bf16), jnp.where(keep, xp, 0.0).astype(bf16),
                                   preferred_element_type=jnp.float32)
                    yp = part if yp is None else yp + part
                pieces.append(yp)
            gcols = slice(g * SSD_GROUP_WIDTH, (g + 1) * SSD_GROUP_WIDTH)
            y = y + jnp.concatenate(pieces, axis=-1) + dsk_ref[:, gcols] * xg
        y_ref[:, g * SSD_GROUP_WIDTH:(g + 1) * SSD_GROUP_WIDTH] = y


def _ssd_scan(proj, conv_w, conv_b, dt_bias, a_log, d_skip, reverse):
    b, s, _ = proj.shape
    nc = s // SSD_CHUNK
    z_end = 3 * ATTN_WIDTH + SSD_WIDTH
    xbc_blk = z_end // SSD_CONV_CH
    dt_blk = (z_end + SSD_CONV_CH) // LANES
    assert z_end % SSD_CONV_CH == 0 and (z_end + SSD_CONV_CH) % LANES == 0
    halo_per_chunk = SSD_CHUNK // SSD_HALO
    n_halo = s // SSD_HALO

    def chunk(i):
        return (nc - 1 - i) if reverse else i

    cw = jnp.pad(conv_w, ((0, SSD_HALO - SSD_CONV), (0, 0)))
    pad16 = lambda v: jnp.pad(v, (0, LANES - 2 * SSD_HEADS)).reshape(1, LANES)
    vec = lambda width: pl.BlockSpec((1, width), lambda bi, i: (0, 0))
    return pl.pallas_call(
        functools.partial(_ssd_kernel, reverse=reverse),
        grid=(b, nc),
        in_specs=[
            pl.BlockSpec((None, SSD_CHUNK, SSD_CONV_CH), lambda bi, i: (bi, chunk(i), xbc_blk)),
            pl.BlockSpec((None, SSD_HALO, SSD_CONV_CH),
                         lambda bi, i: (bi, jnp.maximum(chunk(i) * halo_per_chunk - 1, 0), xbc_blk)),
            pl.BlockSpec((None, SSD_HALO, SSD_CONV_CH),
                         lambda bi, i: (bi, jnp.minimum((chunk(i) + 1) * halo_per_chunk, n_halo - 1),
                                        xbc_blk)),
            pl.BlockSpec((None, SSD_CHUNK, LANES), lambda bi, i: (bi, chunk(i), dt_blk)),
            pl.BlockSpec((SSD_HALO, SSD_CONV_CH), lambda bi, i: (0, 0)),
            vec(SSD_CONV_CH), vec(LANES), vec(LANES), vec(SSD_WIDTH),
        ],
        out_specs=pl.BlockSpec((None, SSD_CHUNK, SSD_WIDTH), lambda bi, i: (bi, chunk(i), 0)),
        out_shape=jax.ShapeDtypeStruct((b, s, SSD_WIDTH), jnp.float32),
        scratch_shapes=[
            pltpu.VMEM((SSD_CHUNK + 2 * SSD_HALO, SSD_CONV_CH), jnp.float32),
            pltpu.VMEM((SSD_CHUNK, SSD_CONV_CH), jnp.float32),
            pltpu.VMEM((SSD_GROUPS, SSD_STATE, SSD_GROUP_WIDTH), jnp.float32),
        ],
        compiler_params=pltpu.CompilerParams(
            dimension_semantics=("arbitrary", "arbitrary"),
            vmem_limit_bytes=32 * 1024 * 1024),
        name="ssd_bwd" if reverse else "ssd_fwd",
    )(proj, proj, proj, proj, cw, conv_b.reshape(1, SSD_CONV_CH), pad16(dt_bias), pad16(a_log),
      jnp.repeat(d_skip, SSD_HEAD_DIM).reshape(1, SSD_WIDTH))


def _ssd_pre_gate(proj, conv_w, conv_b, dt_bias_fwd, a_log_fwd, dt_bias_bwd, a_log_bwd, d_skip):
    dt_bias = jnp.concatenate([dt_bias_fwd, dt_bias_bwd])
    a_log = jnp.concatenate([a_log_fwd, a_log_bwd])
    y_f = _ssd_scan(proj, conv_w, conv_b, dt_bias, a_log, d_skip, reverse=False)
    y_b = _ssd_scan(proj, conv_w, conv_b, dt_bias, a_log, d_skip, reverse=True)
    return y_f, y_b


def _mix_kernel(attn_ref, yf_ref, yb_ref, z_ref, x_ref, anw_ref, snw_ref, wout_ref,
                l1w_ref, l1b_ref, rw_hi_ref, rw_lo_ref, rb_ref, tri_ref,
                h_ref, e_ref, g_ref, r_ref, cnt_ref, carry):
    bf16 = jnp.bfloat16

    @pl.when(pl.program_id(0) == 0)
    def _():
        carry[...] = jnp.zeros_like(carry)

    a = attn_ref[...]
    a = a * lax.rsqrt(jnp.mean(a * a, axis=-1, keepdims=True) + NORM_EPS) * anw_ref[...]
    z = z_ref[...]
    y = (yf_ref[...] + yb_ref[...]) * (z * jax.nn.sigmoid(z))
    groups = []
    for g in range(SSD_GROUPS):
        yg = y[:, g * SSD_GROUP_WIDTH:(g + 1) * SSD_GROUP_WIDTH]
        groups.append(yg * lax.rsqrt(jnp.mean(yg * yg, axis=-1, keepdims=True) + NORM_EPS))
    y = jnp.concatenate(groups, axis=-1) * snw_ref[...]
    mix = (jnp.dot(a.astype(bf16), wout_ref[0:ATTN_WIDTH, :], preferred_element_type=jnp.float32)
           + jnp.dot(y.astype(bf16), wout_ref[ATTN_WIDTH:, :], preferred_element_type=jnp.float32))
    v = DEEPNORM_ALPHA * x_ref[...] + mix
    mu = jnp.mean(v, axis=-1, keepdims=True)
    var = jnp.mean(jnp.square(v - mu), axis=-1, keepdims=True)
    h1 = (v - mu) * lax.rsqrt(var + NORM_EPS) * l1w_ref[...] + l1b_ref[...]
    for c in range(SLAB):
        h_ref[pl.ds(c, MIX_ROWS, stride=SLAB), :] = h1[:, c * LANES:(c + 1) * LANES]

    h_hi = h1.astype(bf16)
    h_lo = (h1 - h_hi.astype(jnp.float32)).astype(bf16)
    logits = (jnp.dot(h_hi, rw_hi_ref[...], preferred_element_type=jnp.float32)
              + jnp.dot(h_lo, rw_hi_ref[...], preferred_element_type=jnp.float32)
              + jnp.dot(h_hi, rw_lo_ref[...], preferred_element_type=jnp.float32)
              + rb_ref[...])
    lane = lax.broadcasted_iota(jnp.int32, (MIX_ROWS, LANES), 1)
    lane_f = lane.astype(jnp.float32)
    cur = jnp.where(lane < N_EXPERTS, logits, -jnp.inf)
    vals, idxs = [], []
    for _ in range(TOP_K):
        m = jnp.max(cur, axis=-1, keepdims=True)
        idx = jnp.min(jnp.where(cur == m, lane_f, float(LANES)), axis=-1, keepdims=True)
        vals.append(m)
        idxs.append(idx)
        cur = jnp.where(lane_f == idx, -jnp.inf, cur)
    probs = [jnp.exp(val - vals[0]) for val in vals]
    den = probs[0]
    for p in probs[1:]:
        den = den + p

    sel = jnp.zeros((MIX_ROWS, LANES), jnp.float32)
    for k in range(TOP_K):
        sel = sel + jnp.where(lane_f == idxs[k], 1.0, 0.0)
    rank_all = jnp.dot(tri_ref[...], sel.astype(bf16), preferred_element_type=jnp.float32) \
        + carry[...]
    e_out = jnp.zeros((MIX_ROWS, LANES), jnp.float32)
    g_out = jnp.zeros((MIX_ROWS, LANES), jnp.float32)
    r_out = jnp.zeros((MIX_ROWS, LANES), jnp.float32)
    for k in range(TOP_K):
        rank_k = jnp.sum(jnp.where(lane_f == idxs[k], rank_all, 0.0), axis=-1, keepdims=True)
        e_out = jnp.where(lane == k, idxs[k], e_out)
        g_out = jnp.where(lane == k, probs[k] / den, g_out)
        r_out = jnp.where(lane == k, rank_k, r_out)
    e_ref[...] = e_out.astype(jnp.int32)
    g_ref[...] = g_out
    r_ref[...] = r_out.astype(jnp.int32)
    carry[...] = carry[...] + jnp.sum(sel, axis=0, keepdims=True)
    cnt_ref[...] = jnp.broadcast_to(carry[...], cnt_ref.shape).astype(jnp.int32)


def _mix_ln_router(attn, y_f, y_b, proj, x, attn_norm_w, ssd_norm_w, w_out, ln1_w, ln1_b,
                   router_w, router_b):
    n, d = x.shape
    z_blk = (3 * ATTN_WIDTH) // SSD_WIDTH
    rw = jnp.pad(router_w, ((0, 0), (0, LANES - N_EXPERTS)))
    rw_hi = rw.astype(jnp.bfloat16)
    rw_lo = (rw - rw_hi.astype(jnp.float32)).astype(jnp.bfloat16)
    rb = jnp.pad(router_b, (0, LANES - N_EXPERTS)).reshape(1, LANES)
    tri = (jnp.arange(MIX_ROWS)[None, :] < jnp.arange(MIX_ROWS)[:, None]).astype(jnp.bfloat16)
    half = pl.BlockSpec((MIX_ROWS, ATTN_WIDTH), lambda i: (i, 0))
    vec = lambda width: pl.BlockSpec((1, width), lambda i: (0, 0))
    full = lambda r, c: pl.BlockSpec((r, c), lambda i: (0, 0))
    rows128 = pl.BlockSpec((MIX_ROWS, LANES), lambda i: (i, 0))
    return pl.pallas_call(
        _mix_kernel,
        grid=(n // MIX_ROWS,),
        in_specs=[half, half, half,
                  pl.BlockSpec((MIX_ROWS, SSD_WIDTH), lambda i: (i, z_blk)),
                  pl.BlockSpec((MIX_ROWS, d), lambda i: (i, 0)),
                  vec(ATTN_WIDTH), vec(SSD_WIDTH), full(d, d), vec(d), vec(d),
                  full(d, LANES), full(d, LANES), vec(LANES), full(MIX_ROWS, MIX_ROWS)],
        out_specs=[pl.BlockSpec((MIX_ROWS * SLAB, LANES), lambda i: (i, 0)),
                   rows128, rows128, rows128,
                   pl.BlockSpec((SLAB, LANES), lambda i: (0, 0))],
        out_shape=[jax.ShapeDtypeStruct((n * SLAB, LANES), jnp.float32),
                   jax.ShapeDtypeStruct((n, LANES), jnp.int32),
                   jax.ShapeDtypeStruct((n, LANES), jnp.float32),
                   jax.ShapeDtypeStruct((n, LANES), jnp.int32),
                   jax.ShapeDtypeStruct((SLAB, LANES), jnp.int32)],
        scratch_shapes=[pltpu.VMEM((1, LANES), jnp.float32)],
        compiler_params=pltpu.CompilerParams(
            dimension_semantics=("arbitrary",),
            vmem_limit_bytes=48 * 1024 * 1024),
        name="mix_ln_router",
    )(attn, y_f, y_b, proj, x, attn_norm_w.reshape(1, ATTN_WIDTH), ssd_norm_w.reshape(1, SSD_WIDTH),
      w_out.astype(jnp.bfloat16), ln1_w.reshape(1, d), ln1_b.reshape(1, d), rw_hi, rw_lo, rb, tri)


def _slab_of(row):
    return pl.ds(pl.multiple_of(row * SLAB, SLAB), SLAB)


def _dispatch_kernel(last_ref, has_ref, nt_ref, pos_ref, h_ref, xs_hbm, zbuf, sem, zsem):
    i = pl.program_id(0)
    t_max = xs_hbm.shape[0] // (MOE_TILE * SLAB)

    @pl.when(i == 0)
    def _zero_fill():
        zbuf[...] = jnp.zeros_like(zbuf)

        def fill(tile):
            return pltpu.make_async_copy(
                zbuf, xs_hbm.at[pl.ds(pl.multiple_of(tile * (MOE_TILE * SLAB), MOE_TILE * SLAB),
                                      MOE_TILE * SLAB), :], zsem)

        for e in range(N_EXPERTS):
            @pl.when(has_ref[e] > 0)
            def _(e=e):
                fill(last_ref[e]).start()
        for j in range(N_EXPERTS + 1):
            @pl.when(nt_ref[0] + j < t_max)
            def _(j=j):
                fill(nt_ref[0] + j).start()
        for e in range(N_EXPERTS):
            @pl.when(has_ref[e] > 0)
            def _(e=e):
                fill(last_ref[e]).wait()
        for j in range(N_EXPERTS + 1):
            @pl.when(nt_ref[0] + j < t_max)
            def _(j=j):
                fill(nt_ref[0] + j).wait()

    def copy(k, r):
        return pltpu.make_async_copy(h_ref.at[_slab_of(r), :],
                                     xs_hbm.at[_slab_of(pos_ref[0, 0, k * MIX_ROWS + r]), :], sem)

    def start(r, c):
        for k in range(TOP_K):
            copy(k, r).start()
        return c

    def wait(r, c):
        for k in range(TOP_K):
            copy(k, r).wait()
        return c

    lax.fori_loop(0, MIX_ROWS, start, 0, unroll=4)
    lax.fori_loop(0, MIX_ROWS, wait, 0, unroll=4)


def _dispatch(h1_slab, pos_tab, last_tile, has_tile, n_tiles, n_rows):
    nb = pos_tab.shape[0]
    grid_spec = pltpu.PrefetchScalarGridSpec(
        num_scalar_prefetch=3,
        grid=(nb,),
        in_specs=[pl.BlockSpec((1, 1, TOP_K * MIX_ROWS), lambda i, la, ha, nt: (i, 0, 0),
                               memory_space=pltpu.SMEM),
                  pl.BlockSpec((MIX_ROWS * SLAB, LANES), lambda i, la, ha, nt: (i, 0))],
        out_specs=pl.BlockSpec(memory_space=pl.ANY),
        scratch_shapes=[pltpu.VMEM((MOE_TILE * SLAB, LANES), jnp.float32),
                        pltpu.SemaphoreType.DMA(()),
                        pltpu.SemaphoreType.DMA(())],
    )
    return pl.pallas_call(
        _dispatch_kernel,
        grid_spec=grid_spec,
        out_shape=jax.ShapeDtypeStruct((n_rows * SLAB, LANES), jnp.float32),
        compiler_params=pltpu.CompilerParams(
            dimension_semantics=("arbitrary",),
            vmem_limit_bytes=32 * 1024 * 1024),
        name="dispatch",
    )(last_tile, has_tile, n_tiles, pos_tab, h1_slab)


def _moe_kernel(te_ref, nt_ref, dst_prv_ref,
                x_ref, wg_ref, wu_ref, wd_ref, bg_ref, bu_ref, bd_ref,
                y_hbm,
                ybuf, xb_s, act_s, wg_bf, wu_bf, wd_bf, ssem):
    i = pl.program_id(0)
    n_tiles = nt_ref[0]
    slot = i % 2
    other = 1 - slot
    n_pairs = y_hbm.shape[0] // SLAB - 2 * MOE_TILE
    bf16 = jnp.bfloat16

    def scatter_copy(dst, row, s):
        return pltpu.make_async_copy(ybuf.at[s, _slab_of(row), :],
                                     y_hbm.at[_slab_of(dst), :], ssem.at[s])

    def start_scatter_prev(r):
        dst = jnp.where(i == 0, n_pairs + MOE_TILE + r, dst_prv_ref[0, 0, r])
        scatter_copy(dst, r, other).start()

    def wait_scatter(s):
        def body(r, c):
            scatter_copy(0, r, s).wait()
            return c
        lax.fori_loop(0, MOE_TILE, body, 0, unroll=8)

    @pl.when(i == 0)
    def _first():
        ybuf[...] = jnp.zeros_like(ybuf)
        for half in range(2):
            cp = pltpu.make_async_copy(
                ybuf.at[0],
                y_hbm.at[pl.ds((n_pairs + half * MOE_TILE) * SLAB, MOE_TILE * SLAB), :],
                ssem.at[half])
            cp.start()
            cp.wait()

    @pl.when((i >= 1) & (i <= n_tiles))
    def _arrive():
        wait_scatter(slot)

    @pl.when(i < n_tiles)
    def _compute():
        e = te_ref[i]
        prev_e = te_ref[jnp.maximum(i - 1, 0)]

        @pl.when((i == 0) | (e != prev_e))
        def _():
            wg_bf[...] = wg_ref[0].astype(bf16)
            wu_bf[...] = wu_ref[0].astype(bf16)
            wd_bf[...] = wd_ref[0].astype(bf16)

        for c in range(SLAB):
            xb_s[:, c * LANES:(c + 1) * LANES] = \
                x_ref[pl.ds(c, MOE_TILE, stride=SLAB), :].astype(bf16)

        batch = MOE_TILE // (2 * (D_MODEL // FFN_CHUNK))
        for c in range(D_MODEL // FFN_CHUNK):
            cols = slice(c * FFN_CHUNK, (c + 1) * FFN_CHUNK)
            g = jnp.dot(xb_s[...], wg_bf[:, cols], preferred_element_type=jnp.float32)
            g = g + bg_ref[pl.ds(e, 1), cols]
            for r in range(2 * c * batch, (2 * c + 1) * batch):
                start_scatter_prev(r)
            u = jnp.dot(xb_s[...], wu_bf[:, cols], preferred_element_type=jnp.float32)
            u = u + bu_ref[pl.ds(e, 1), cols]
            for r in range((2 * c + 1) * batch, (2 * c + 2) * batch):
                start_scatter_prev(r)
            g = jnp.minimum(g, SWIGLU_LIMIT)
            u = jnp.clip(u, -SWIGLU_LIMIT, SWIGLU_LIMIT)
            act_s[:, cols] = ((u + 1.0) * g * jax.nn.sigmoid(SWIGLU_ALPHA * g)).astype(bf16)

        for c in range(D_MODEL // FFN_CHUNK):
            cols = slice(c * FFN_CHUNK, (c + 1) * FFN_CHUNK)
            y = jnp.dot(act_s[...], wd_bf[:, cols], preferred_element_type=jnp.float32)
            y = y + bd_ref[pl.ds(e, 1), cols]
            for j in range(FFN_CHUNK // LANES):
                ybuf[slot, pl.ds(c * (FFN_CHUNK // LANES) + j, MOE_TILE, stride=SLAB), :] = \
                    y[:, j * LANES:(j + 1) * LANES]

    @pl.when(i == n_tiles)
    def _drain():
        def body(r, c):
            start_scatter_prev(r)
            return c
        lax.fori_loop(0, MOE_TILE, body, 0, unroll=8)
        wait_scatter(other)


def _moe_ffn(xs, tile_e, n_tiles, dst_row, w_gate, b_gate, w_up, b_up, w_down, b_down,
             n_out_rows):
    t_max = tile_e.shape[0]
    d = D_MODEL
    w_spec = pl.BlockSpec((1, d, d), lambda i, te, nt: (te[i], 0, 0))
    b_spec = pl.BlockSpec((N_EXPERTS, d), lambda i, te, nt: (0, 0))
    grid_spec = pltpu.PrefetchScalarGridSpec(
        num_scalar_prefetch=2,
        grid=(t_max,),
        in_specs=[pl.BlockSpec((1, 1, MOE_TILE), lambda i, te, nt: (jnp.maximum(i - 1, 0), 0, 0),
                               memory_space=pltpu.SMEM),
                  pl.BlockSpec((MOE_TILE * SLAB, LANES),
                               lambda i, te, nt: (jnp.minimum(i, nt[0] - 1), 0)),
                  w_spec, w_spec, w_spec, b_spec, b_spec, b_spec],
        out_specs=pl.BlockSpec(memory_space=pl.ANY),
        scratch_shapes=[
            pltpu.VMEM((2, MOE_TILE * SLAB, LANES), jnp.float32),
            pltpu.VMEM((MOE_TILE, d), jnp.bfloat16),
            pltpu.VMEM((MOE_TILE, d), jnp.bfloat16),
            pltpu.VMEM((d, d), jnp.bfloat16),
            pltpu.VMEM((d, d), jnp.bfloat16),
            pltpu.VMEM((d, d), jnp.bfloat16),
            pltpu.SemaphoreType.DMA((2,)),
        ],
    )
    return pl.pallas_call(
        _moe_kernel,
        grid_spec=grid_spec,
        out_shape=jax.ShapeDtypeStruct((n_out_rows * SLAB, LANES), jnp.float32),
        compiler_params=pltpu.CompilerParams(
            dimension_semantics=("arbitrary",),
            vmem_limit_bytes=V7X_VMEM_BYTES - 6 * 1024 * 1024),
        name="moe_ffn",
    )(tile_e, n_tiles, dst_row, xs, w_gate, w_up, w_down, b_gate, b_up, b_down)


def _combine_ln_kernel(y0_ref, y1_ref, y2_ref, y3_ref, h_ref, g_ref, w_ref, b_ref, o_ref):
    g = g_ref[...]
    ffn = (g[:, 0:1] * _slab_rows(y0_ref, LN_ROWS) + g[:, 1:2] * _slab_rows(y1_ref, LN_ROWS)
           + g[:, 2:3] * _slab_rows(y2_ref, LN_ROWS) + g[:, 3:4] * _slab_rows(y3_ref, LN_ROWS))
    v = DEEPNORM_ALPHA * _slab_rows(h_ref, LN_ROWS) + ffn
    mu = jnp.mean(v, axis=-1, keepdims=True)
    var = jnp.mean(jnp.square(v - mu), axis=-1, keepdims=True)
    o_ref[...] = (v - mu) * lax.rsqrt(var + NORM_EPS) * w_ref[...] + b_ref[...]


def _combine_ln(y_planes, h1_slab, gates, ln_w, ln_b):
    n = gates.shape[0]
    d = D_MODEL
    nb = n // LN_ROWS

    def plane(k):
        return pl.BlockSpec((LN_ROWS * SLAB, LANES), lambda i, k=k: (k * nb + i, 0))

    vec = pl.BlockSpec((1, d), lambda i: (0, 0))
    return pl.pallas_call(
        _combine_ln_kernel,
        grid=(nb,),
        in_specs=[plane(0), plane(1), plane(2), plane(3), plane(0),
                  pl.BlockSpec((LN_ROWS, LANES), lambda i: (i, 0)), vec, vec],
        out_specs=pl.BlockSpec((LN_ROWS, d), lambda i: (i, 0)),
        out_shape=jax.ShapeDtypeStruct((n, d), jnp.float32),
        compiler_params=pltpu.CompilerParams(
            dimension_semantics=("arbitrary",),
            vmem_limit_bytes=48 * 1024 * 1024),
        name="combine_ln",
    )(y_planes, y_planes, y_planes, y_planes, h1_slab, gates,
      ln_w.reshape(1, d), ln_b.reshape(1, d))


def _routing_tables(top_e, rank, counts, n_tok):
    n_pairs = TOP_K * n_tok
    t_max = n_pairs // MOE_TILE + N_EXPERTS + 1
    n_rows = t_max * MOE_TILE
    tiles_per_e = (counts + MOE_TILE - 1) // MOE_TILE
    tile_end = jnp.cumsum(tiles_per_e)
    tile_start = tile_end - tiles_per_e
    n_tiles = tile_end[-1]
    tile_ids = jnp.arange(t_max, dtype=jnp.int32)
    tile_e = jnp.sum((tile_end[None, :] <= jnp.minimum(tile_ids, n_tiles - 1)[:, None])
                     .astype(jnp.int32), axis=1)
    tile_e = jnp.minimum(tile_e, N_EXPERTS - 1).astype(jnp.int32)

    base = (tile_start * MOE_TILE).astype(jnp.int32)
    experts = jnp.arange(N_EXPERTS, dtype=jnp.int32)
    pos = jnp.sum(jnp.where(top_e[:, :, None] == experts[None, None, :], base[None, None, :], 0),
                  axis=-1) + rank
    pair_id = (jnp.arange(n_tok, dtype=jnp.int32)[:, None]
               + n_tok * jnp.arange(TOP_K, dtype=jnp.int32)[None, :])
    _, pairs_by_row = lax.sort_key_val(pos.reshape(-1), pair_id.reshape(-1))
    first = (jnp.cumsum(counts) - counts).astype(jnp.int32)
    lane = jnp.arange(MOE_TILE, dtype=jnp.int32)[None, :]
    row_in_e = (tile_ids - tile_start[tile_e])[:, None] * MOE_TILE + lane
    valid = (row_in_e < counts[tile_e][:, None]) & (tile_ids < n_tiles)[:, None]
    owner = pairs_by_row[jnp.clip(first[tile_e][:, None] + row_in_e, 0, n_pairs - 1)]
    spare = n_pairs + (tile_ids % 2)[:, None] * MOE_TILE + lane
    dst_row = jnp.where(valid, owner, spare).astype(jnp.int32).reshape(t_max, 1, MOE_TILE)

    nb = n_tok // MIX_ROWS
    pos_tab = pos.reshape(nb, MIX_ROWS, TOP_K).transpose(0, 2, 1).reshape(nb, 1, TOP_K * MIX_ROWS)
    last_tile = jnp.maximum(tile_end - 1, 0).astype(jnp.int32)
    has_tile = (tiles_per_e > 0).astype(jnp.int32)
    return (tile_e, n_tiles.reshape(1).astype(jnp.int32), pos_tab, dst_row, last_tile, has_tile,
            n_rows)


def kernel(x, w_in, attn_norm_w, conv_w, conv_b, dt_bias_fwd, a_log_fwd, dt_bias_bwd,
           a_log_bwd, d_skip, ssd_norm_w, w_out, ln1_w, ln1_b, router_w, router_b,
           w_gate, b_gate, w_up, b_up, w_down, b_down, ln2_w, ln2_b):
    b, s, d = x.shape
    n = b * s
    xf = x.reshape(n, d)
    in_cols = w_in.shape[-1]
    in_cols_pad = -(-in_cols // LANES) * LANES

    w_in_b = jnp.pad(w_in[0], ((0, 0), (0, in_cols_pad - in_cols))).astype(jnp.bfloat16)
    proj = _rows_matmul(xf, w_in_b, "in_proj")
    proj3 = proj.reshape(b, s, in_cols_pad)
    attn = _attention(proj3)
    y_f, y_b = _ssd_pre_gate(proj3, conv_w[0], conv_b[0], dt_bias_fwd[0], a_log_fwd[0],
                             dt_bias_bwd[0], a_log_bwd[0], d_skip[0])
    h1_slab, top_e, gates, rank, counts = _mix_ln_router(
        attn.reshape(n, ATTN_WIDTH), y_f.reshape(n, SSD_WIDTH), y_b.reshape(n, SSD_WIDTH),
        proj, xf, attn_norm_w[0], ssd_norm_w[0], w_out[0], ln1_w[0], ln1_b[0],
        router_w[0], router_b[0])

    tile_e, n_tiles, pos_tab, dst_row, last_tile, has_tile, n_rows = _routing_tables(
        top_e[:, :TOP_K], rank[:, :TOP_K], counts[0, :N_EXPERTS], n)
    xs = _dispatch(h1_slab, pos_tab, last_tile, has_tile, n_tiles, n_rows)
    n_out_rows = TOP_K * n + 2 * MOE_TILE
    y_planes = _moe_ffn(xs, tile_e, n_tiles, dst_row,
                        w_gate[0], b_gate[0], w_up[0], b_up[0], w_down[0], b_down[0],
                        n_out_rows)
    out = _combine_ln(y_planes, h1_slab, gates, ln2_w[0], ln2_b[0])
    return out.reshape(b, s, d)
```

```python
import functools

import jax
import jax.numpy as jnp
from jax import lax
from jax.experimental import pallas as pl
from jax.experimental.pallas import tpu as pltpu

D_MODEL = 1024
HEAD_DIM = 64
ATTN_HEADS = 8
ATTN_WIDTH = ATTN_HEADS * HEAD_DIM
DILATED_PATTERNS = ((128, 1), (512, 4), (2048, 16))
ROPE_THETA = 10000.0
SSD_HEADS = 8
SSD_HEAD_DIM = 64
SSD_WIDTH = SSD_HEADS * SSD_HEAD_DIM
SSD_GROUPS = 2
SSD_STATE = 128
SSD_CONV = 5
SSD_CHUNK = 128
SSD_CONV_CH = SSD_WIDTH + 2 * SSD_GROUPS * SSD_STATE
N_EXPERTS = 32
TOP_K = 4
SWIGLU_ALPHA = 1.702
SWIGLU_LIMIT = 7.0
DEPTH = 1
DEEPNORM_ALPHA = (2.0 * DEPTH) ** 0.25
NORM_EPS = 1e-5

V7X_VMEM_BYTES = 64 * 1024 * 1024
LANES = 128

MATMUL_ROWS = 512
MIX_ROWS = 512
MOE_TILE = 512
FFN_CHUNK = 256
LN_ROWS = 512
SLAB = D_MODEL // LANES
PACK = SLAB // 2


def _slab_rows(ref, rows):
    return jnp.concatenate(
        [ref[pl.ds(c, rows, stride=SLAB), :] for c in range(SLAB)], axis=-1)


def _pack_pair(lo, hi):
    lo_bits = lax.bitcast_convert_type(lo.astype(jnp.bfloat16).astype(jnp.float32), jnp.uint32)
    hi_bits = lax.bitcast_convert_type(hi.astype(jnp.bfloat16).astype(jnp.float32), jnp.uint32)
    return (lo_bits >> 16) | (hi_bits & jnp.uint32(0xFFFF0000))


def _unpack_pair(word):
    lo = lax.bitcast_convert_type(word << 16, jnp.float32)
    hi = lax.bitcast_convert_type(word & jnp.uint32(0xFFFF0000), jnp.float32)
    return lo, hi


def _store_packed(ref, rows, value):
    for c in range(PACK):
        lo = value[:, (2 * c) * LANES:(2 * c + 1) * LANES]
        hi = value[:, (2 * c + 1) * LANES:(2 * c + 2) * LANES]
        ref[pl.ds(c, rows, stride=PACK), :] = _pack_pair(lo, hi)


def _rows_matmul_kernel(x_ref, w_ref, o_ref):
    o_ref[...] = jnp.dot(x_ref[...].astype(jnp.bfloat16), w_ref[...],
                         preferred_element_type=jnp.float32)


def _rows_matmul(x, w_bf16, name):
    n, k = x.shape
    _, m = w_bf16.shape
    assert n % MATMUL_ROWS == 0 and m % LANES == 0
    return pl.pallas_call(
        _rows_matmul_kernel,
        grid=(n // MATMUL_ROWS,),
        in_specs=[pl.BlockSpec((MATMUL_ROWS, k), lambda i: (i, 0)),
                  pl.BlockSpec((k, m), lambda i: (0, 0))],
        out_specs=pl.BlockSpec((MATMUL_ROWS, m), lambda i: (i, 0)),
        out_shape=jax.ShapeDtypeStruct((n, m), jnp.float32),
        compiler_params=pltpu.CompilerParams(
            dimension_semantics=("arbitrary",),
            vmem_limit_bytes=48 * 1024 * 1024),
        name=name,
    )(x, w_bf16)


ATTN_BLOCK = 128
ATTN_WINDOW = 256
ATTN_HALF = 64
ATTN_STAGE_ROWS = 256


def _rope_tables(s):
    pos = jnp.arange(s, dtype=jnp.float32)
    inv_freq = ROPE_THETA ** (-jnp.arange(0, HEAD_DIM, 2, dtype=jnp.float32) / HEAD_DIM)
    ang = pos[:, None] * inv_freq[None, :]
    return jnp.cos(ang), jnp.sin(ang)


def _strided_rows(start, size, stride):
    return pl.ds(start, size) if stride == 1 else pl.ds(start, size, stride=stride)


def _attention_kernel(q_ref, k_ref, v_ref, cos_ref, sin_lo_ref, sin_hi_ref, o_ref,
                      qr, kr, q0s, q1s, ks, vs, acc_o, acc_m, acc_l):
    s_len = q_ref.shape[0]
    head0 = lax.broadcasted_iota(jnp.int32, (1, LANES), 1) < HEAD_DIM
    q_i = lax.broadcasted_iota(jnp.int32, (ATTN_BLOCK, ATTN_WINDOW), 0)
    k_j = lax.broadcasted_iota(jnp.int32, (ATTN_BLOCK, ATTN_WINDOW), 1)
    rel = q_i - k_j
    last = len(DILATED_PATTERNS) - 1

    def rotate(j, carry):
        rows = pl.ds(pl.multiple_of(j * ATTN_STAGE_ROWS, ATTN_STAGE_ROWS), ATTN_STAGE_ROWS)
        cos = cos_ref[rows, :]
        sin_lo = sin_lo_ref[rows, :]
        sin_hi = sin_hi_ref[rows, :]

        def rope(t):
            return (t * cos + pltpu.roll(t, LANES - HEAD_DIM // 2, 1) * sin_lo
                    + pltpu.roll(t, HEAD_DIM // 2, 1) * sin_hi)

        qr[rows, :] = rope(q_ref[rows, :]) * (HEAD_DIM ** -0.5)
        kr[rows, :] = rope(k_ref[rows, :])
        return carry

    lax.fori_loop(0, s_len // ATTN_STAGE_ROWS, rotate, 0, unroll=2)

    for pi, (window, dil) in enumerate(DILATED_PATTERNS):
        assert window // (2 * dil) == ATTN_HALF
        row_len = s_len // dil
        assert row_len % ATTN_WINDOW == 0 and row_len % ATTN_STAGE_ROWS == 0

        def stage(j, carry, dil=dil, row_len=row_len):
            f0 = pl.multiple_of(j * ATTN_STAGE_ROWS, ATTN_STAGE_ROWS)
            r = f0 // row_len
            src = _strided_rows((f0 - r * row_len) * dil + r, ATTN_STAGE_ROWS, dil)
            dst = pl.ds(f0, ATTN_STAGE_ROWS)
            qv = qr[src, :]
            q0s[dst, :] = jnp.where(head0, qv, 0.0).astype(jnp.bfloat16)
            q1s[dst, :] = jnp.where(head0, 0.0, qv).astype(jnp.bfloat16)
            ks[dst, :] = kr[src, :].astype(jnp.bfloat16)
            vs[dst, :] = v_ref[src, :].astype(jnp.bfloat16)
            return carry

        lax.fori_loop(0, s_len // ATTN_STAGE_ROWS, stage, 0, unroll=2)

        def block(bi, carry, pi=pi, dil=dil, row_len=row_len):
            f0 = pl.multiple_of(bi * ATTN_BLOCK, ATTN_BLOCK)
            r = f0 // row_len
            row_start = r * row_len
            win = jnp.clip(f0 - ATTN_HALF, row_start, row_start + row_len - ATTN_WINDOW)
            win = pl.multiple_of(win, ATTN_HALF)
            valid = jnp.abs(rel + (f0 - win)) <= ATTN_HALF
            kw = ks[pl.ds(win, ATTN_WINDOW), :]
            vw = vs[pl.ds(win, ATTN_WINDOW), :]
            parts = []
            for qs_ref in (q0s, q1s):
                qb = qs_ref[pl.ds(f0, ATTN_BLOCK), :]
                sc = lax.dot_general(qb, kw, (((1,), (1,)), ((), ())),
                                     preferred_element_type=jnp.float32)
                sc = jnp.where(valid, sc, -jnp.inf)
                m = jnp.max(sc, axis=-1, keepdims=True)
                p = jnp.exp(sc - m)
                den = jnp.sum(p, axis=-1, keepdims=True)
                num = jnp.dot(p.astype(jnp.bfloat16), vw, preferred_element_type=jnp.float32)
                parts.append((num, m, den))
            num = jnp.where(head0, parts[0][0], parts[1][0])
            m = jnp.where(head0, parts[0][1], parts[1][1])
            den = jnp.where(head0, parts[0][2], parts[1][2])
            rows = _strided_rows((f0 - row_start) * dil + r, ATTN_BLOCK, dil)
            if pi == 0:
                acc_o[rows, :] = num
                acc_m[rows, :] = m
                acc_l[rows, :] = den
            else:
                m_old = acc_m[rows, :]
                m_new = jnp.maximum(m_old, m)
                w_old = jnp.exp(m_old - m_new)
                w_cur = jnp.exp(m - m_new)
                num = acc_o[rows, :] * w_old + num * w_cur
                den = acc_l[rows, :] * w_old + den * w_cur
                if pi == last:
                    o_ref[rows, :] = num / den
                else:
                    acc_o[rows, :] = num
                    acc_m[rows, :] = m_new
                    acc_l[rows, :] = den
            return carry

        lax.fori_loop(0, s_len // ATTN_BLOCK, block, 0, unroll=4)


def _attention(proj):
    b, s, _ = proj.shape
    cos, sin = _rope_tables(s)
    reps = LANES // (HEAD_DIM // 2)
    first_half = (jnp.arange(LANES) % HEAD_DIM) < HEAD_DIM // 2
    cos_t = jnp.tile(cos, (1, reps))
    sin_t = jnp.tile(sin, (1, reps))
    sin_lo = jnp.where(first_half, -sin_t, 0.0)
    sin_hi = jnp.where(first_half, 0.0, sin_t)
    pairs = ATTN_WIDTH // LANES

    def cols(off):
        return pl.BlockSpec((None, s, LANES), lambda bi, hp, off=off: (bi, 0, off + hp))

    table = pl.BlockSpec((s, LANES), lambda bi, hp: (0, 0))
    return pl.pallas_call(
        _attention_kernel,
        grid=(b, pairs),
        in_specs=[cols(0), cols(pairs), cols(2 * pairs), table, table, table],
        out_specs=pl.BlockSpec((None, s, LANES), lambda bi, hp: (bi, 0, hp)),
        out_shape=jax.ShapeDtypeStruct((b, s, ATTN_WIDTH), jnp.float32),
        scratch_shapes=[pltpu.VMEM((s, LANES), jnp.float32)] * 2
        + [pltpu.VMEM((s, LANES), jnp.bfloat16)] * 4
        + [pltpu.VMEM((s, LANES), jnp.float32)] * 3,
        compiler_params=pltpu.CompilerParams(
            dimension_semantics=("arbitrary", "arbitrary"),
            vmem_limit_bytes=56 * 1024 * 1024),
        name="dilated_attention",
    )(proj, proj, proj, cos_t, sin_lo, sin_hi)


SSD_HALO = 8
SSD_GROUP_HEADS = SSD_HEADS // SSD_GROUPS
SSD_GROUP_WIDTH = SSD_GROUP_HEADS * SSD_HEAD_DIM


def _softplus(v):
    return jnp.maximum(v, 0.0) + jnp.log(1.0 + jnp.exp(-jnp.abs(v)))


def _expand_heads(mat, first_head):
    t = mat.shape[0]
    low = lax.broadcasted_iota(jnp.int32, (1, LANES), 1) < SSD_HEAD_DIM
    tiles = []
    for j in range(0, SSD_GROUP_HEADS, 2):
        a = jnp.broadcast_to(mat[:, first_head + j:first_head + j + 1], (t, LANES))
        b = jnp.broadcast_to(mat[:, first_head + j + 1:first_head + j + 2], (t, LANES))
        tiles.append(jnp.where(low, a, b))
    return jnp.concatenate(tiles, axis=-1)


def _ssd_kernel(xbc_ref, prev_ref, next_ref, dt_ref, cw_ref, cb_ref, dtb_ref, alog_ref, dsk_ref,
                y_ref, ext, cv, state, *, reverse):
    i = pl.program_id(1)
    nc = pl.num_programs(1)
    c = (nc - 1 - i) if reverse else i
    t = SSD_CHUNK
    bf16 = jnp.bfloat16

    @pl.when(i == 0)
    def _():
        state[...] = jnp.zeros_like(state)

    ext[0:SSD_HALO, :] = jnp.where(c > 0, prev_ref[...], 0.0)
    ext[SSD_HALO:SSD_HALO + t, :] = xbc_ref[...]
    ext[SSD_HALO + t:, :] = jnp.where(c < nc - 1, next_ref[...], 0.0)
    for j in range(SSD_CONV_CH // LANES):
        cols = slice(j * LANES, (j + 1) * LANES)
        acc = jnp.broadcast_to(cb_ref[:, cols], (t, LANES))
        for k in range(SSD_CONV):
            acc = acc + cw_ref[k:k + 1, cols] * ext[pl.ds(SSD_HALO - SSD_CONV // 2 + k, t), cols]
        cv[:, cols] = acc * jax.nn.sigmoid(acc)

    dt = _softplus(dt_ref[...] + dtb_ref[...])
    a = dt * (-jnp.exp(alog_ref[...]))
    row = lax.broadcasted_iota(jnp.int32, (t, LANES), 0)
    lane = lax.broadcasted_iota(jnp.int32, (t, LANES), 1)
    cs = a
    shift = 1
    while shift < t:
        cs = cs + jnp.where(row >= shift, pltpu.roll(cs, shift, 0), 0.0)
        shift *= 2
    total = cs[t - 1:t, :]
    e = jnp.where(lane < SSD_HEADS, cs, total - cs + a)
    off = SSD_HEADS if reverse else 0
    edge = e[0:1, :] if reverse else e[t - 1:t, :]
    scale_y = jnp.exp(e)
    scale_s = jnp.exp(edge - e) * dt

    x_off = 0
    b_off = SSD_WIDTH
    c_off = SSD_WIDTH + SSD_GROUPS * SSD_STATE
    low = lax.broadcasted_iota(jnp.int32, (1, LANES), 1) < SSD_HEAD_DIM

    if not reverse:
        e_t = e.T
        dt_t = dt.T
        li = lax.broadcasted_iota(jnp.int32, (t, t), 0)
        si = lax.broadcasted_iota(jnp.int32, (t, t), 1)
        lower = si <= li
        upper = si >= li

    for g in range(SSD_GROUPS):
        bg = cv[:, b_off + g * SSD_STATE:b_off + (g + 1) * SSD_STATE].astype(bf16)
        cg = cv[:, c_off + g * SSD_STATE:c_off + (g + 1) * SSD_STATE].astype(bf16)
        xg = cv[:, x_off + g * SSD_GROUP_WIDTH:x_off + (g + 1) * SSD_GROUP_WIDTH]
        h0 = g * SSD_GROUP_HEADS
        st = state[g]
        y = jnp.dot(cg, st.astype(bf16), preferred_element_type=jnp.float32) \
            * _expand_heads(scale_y, off + h0)
        xw = (xg * _expand_heads(scale_s, off + h0)).astype(bf16)
        new = lax.dot_general(bg, xw, (((0,), (0,)), ((), ())),
                              preferred_element_type=jnp.float32)
        decay = _expand_heads(jnp.broadcast_to(jnp.exp(edge), (SSD_HALO, LANES)), off + h0)
        state[g] = st * decay[0:1, :] + new

        if not reverse:
            cbm = lax.dot_general(cg, bg, (((1,), (1,)), ((), ())),
                                  preferred_element_type=jnp.float32)
            pieces = []
            for j in range(0, SSD_GROUP_HEADS, 2):
                xp = xg[:, j * SSD_HEAD_DIM:(j + 2) * SSD_HEAD_DIM]
                yp = None
                for hh, keep in ((j, low), (j + 1, jnp.logical_not(low))):
                    h = h0 + hh
                    hb = SSD_HEADS + h
                    lf = jnp.exp(jnp.where(lower, e[:, h:h + 1] - e_t[h:h + 1, :], -jnp.inf))
                    lb = jnp.exp(jnp.where(upper, e[:, hb:hb + 1] - e_t[hb:hb + 1, :], -jnp.inf))
                    mh = cbm * (lf * dt_t[h:h + 1, :] + lb * dt_t[hb:hb + 1, :])
                    part = jnp.dot(mh.astype(bf16), jnp.where(keep, xp, 0.0).astype(bf16),
                                   preferred_element_type=jnp.float32)
                    yp = part if yp is None else yp + part
                pieces.append(yp)
            gcols = slice(g * SSD_GROUP_WIDTH, (g + 1) * SSD_GROUP_WIDTH)
            y = y + jnp.concatenate(pieces, axis=-1) + dsk_ref[:, gcols] * xg
        y_ref[:, g * SSD_GROUP_WIDTH:(g + 1) * SSD_GROUP_WIDTH] = y


def _ssd_scan(proj, conv_w, conv_b, dt_bias, a_log, d_skip, reverse):
    b, s, _ = proj.shape
    nc = s // SSD_CHUNK
    z_end = 3 * ATTN_WIDTH + SSD_WIDTH
    xbc_blk = z_end // SSD_CONV_CH
    dt_blk = (z_end + SSD_CONV_CH) // LANES
    assert z_end % SSD_CONV_CH == 0 and (z_end + SSD_CONV_CH) % LANES == 0
    halo_per_chunk = SSD_CHUNK // SSD_HALO
    n_halo = s // SSD_HALO

    def chunk(i):
        return (nc - 1 - i) if reverse else i

    cw = jnp.pad(conv_w, ((0, SSD_HALO - SSD_CONV), (0, 0)))
    pad16 = lambda v: jnp.pad(v, (0, LANES - 2 * SSD_HEADS)).reshape(1, LANES)
    vec = lambda width: pl.BlockSpec((1, width), lambda bi, i: (0, 0))
    return pl.pallas_call(
        functools.partial(_ssd_kernel, reverse=reverse),
        grid=(b, nc),
        in_specs=[
            pl.BlockSpec((None, SSD_CHUNK, SSD_CONV_CH), lambda bi, i: (bi, chunk(i), xbc_blk)),
            pl.BlockSpec((None, SSD_HALO, SSD_CONV_CH),
                         lambda bi, i: (bi, jnp.maximum(chunk(i) * halo_per_chunk - 1, 0), xbc_blk)),
            pl.BlockSpec((None, SSD_HALO, SSD_CONV_CH),
                         lambda bi, i: (bi, jnp.minimum((chunk(i) + 1) * halo_per_chunk, n_halo - 1),
                                        xbc_blk)),
            pl.BlockSpec((None, SSD_CHUNK, LANES), lambda bi, i: (bi, chunk(i), dt_blk)),
            pl.BlockSpec((SSD_HALO, SSD_CONV_CH), lambda bi, i: (0, 0)),
            vec(SSD_CONV_CH), vec(LANES), vec(LANES), vec(SSD_WIDTH),
        ],
        out_specs=pl.BlockSpec((None, SSD_CHUNK, SSD_WIDTH), lambda bi, i: (bi, chunk(i), 0)),
        out_shape=jax.ShapeDtypeStruct((b, s, SSD_WIDTH), jnp.float32),
        scratch_shapes=[
            pltpu.VMEM((SSD_CHUNK + 2 * SSD_HALO, SSD_CONV_CH), jnp.float32),
            pltpu.VMEM((SSD_CHUNK, SSD_CONV_CH), jnp.float32),
            pltpu.VMEM((SSD_GROUPS, SSD_STATE, SSD_GROUP_WIDTH), jnp.float32),
        ],
        compiler_params=pltpu.CompilerParams(
            dimension_semantics=("arbitrary", "arbitrary"),
            vmem_limit_bytes=32 * 1024 * 1024),
        name="ssd_bwd" if reverse else "ssd_fwd",
    )(proj, proj, proj, proj, cw, conv_b.reshape(1, SSD_CONV_CH), pad16(dt_bias), pad16(a_log),
      jnp.repeat(d_skip, SSD_HEAD_DIM).reshape(1, SSD_WIDTH))


def _ssd_pre_gate(proj, conv_w, conv_b, dt_bias_fwd, a_log_fwd, dt_bias_bwd, a_log_bwd, d_skip):
    dt_bias = jnp.concatenate([dt_bias_fwd, dt_bias_bwd])
    a_log = jnp.concatenate([a_log_fwd, a_log_bwd])
    y_f = _ssd_scan(proj, conv_w, conv_b, dt_bias, a_log, d_skip, reverse=False)
    y_b = _ssd_scan(proj, conv_w, conv_b, dt_bias, a_log, d_skip, reverse=True)
    return y_f, y_b


def _mix_kernel(attn_ref, yf_ref, yb_ref, z_ref, x_ref, anw_ref, snw_ref, wout_ref,
                l1w_ref, l1b_ref, rw_hi_ref, rw_lo_ref, rb_ref, tri_ref,
                h_ref, hp_ref, e_ref, g_ref, r_ref, cnt_ref, carry):
    bf16 = jnp.bfloat16

    @pl.when(pl.program_id(0) == 0)
    def _():
        carry[...] = jnp.zeros_like(carry)

    a = attn_ref[...]
    a = a * lax.rsqrt(jnp.mean(a * a, axis=-1, keepdims=True) + NORM_EPS) * anw_ref[...]
    z = z_ref[...]
    y = (yf_ref[...] + yb_ref[...]) * (z * jax.nn.sigmoid(z))
    groups = []
    for g in range(SSD_GROUPS):
        yg = y[:, g * SSD_GROUP_WIDTH:(g + 1) * SSD_GROUP_WIDTH]
        groups.append(yg * lax.rsqrt(jnp.mean(yg * yg, axis=-1, keepdims=True) + NORM_EPS))
    y = jnp.concatenate(groups, axis=-1) * snw_ref[...]
    mix = (jnp.dot(a.astype(bf16), wout_ref[0:ATTN_WIDTH, :], preferred_element_type=jnp.float32)
           + jnp.dot(y.astype(bf16), wout_ref[ATTN_WIDTH:, :], preferred_element_type=jnp.float32))
    v = DEEPNORM_ALPHA * x_ref[...] + mix
    mu = jnp.mean(v, axis=-1, keepdims=True)
    var = jnp.mean(jnp.square(v - mu), axis=-1, keepdims=True)
    h1 = (v - mu) * lax.rsqrt(var + NORM_EPS) * l1w_ref[...] + l1b_ref[...]
    for c in range(SLAB):
        h_ref[pl.ds(c, MIX_ROWS, stride=SLAB), :] = h1[:, c * LANES:(c + 1) * LANES]
    _store_packed(hp_ref, MIX_ROWS, h1)

    h_hi = h1.astype(bf16)
    h_lo = (h1 - h_hi.astype(jnp.float32)).astype(bf16)
    logits = (jnp.dot(h_hi, rw_hi_ref[...], preferred_element_type=jnp.float32)
              + jnp.dot(h_lo, rw_hi_ref[...], preferred_element_type=jnp.float32)
              + jnp.dot(h_hi, rw_lo_ref[...], preferred_element_type=jnp.float32)
              + rb_ref[...])
    lane = lax.broadcasted_iota(jnp.int32, (MIX_ROWS, LANES), 1)
    lane_f = lane.astype(jnp.float32)
    cur = jnp.where(lane < N_EXPERTS, logits, -jnp.inf)
    vals, idxs = [], []
    for _ in range(TOP_K):
        m = jnp.max(cur, axis=-1, keepdims=True)
        idx = jnp.min(jnp.where(cur == m, lane_f, float(LANES)), axis=-1, keepdims=True)
        vals.append(m)
        idxs.append(idx)
        cur = jnp.where(lane_f == idx, -jnp.inf, cur)
    probs = [jnp.exp(val - vals[0]) for val in vals]
    den = probs[0]
    for p in probs[1:]:
        den = den + p

    sel = jnp.zeros((MIX_ROWS, LANES), jnp.float32)
    for k in range(TOP_K):
        sel = sel + jnp.where(lane_f == idxs[k], 1.0, 0.0)
    rank_all = jnp.dot(tri_ref[...], sel.astype(bf16), preferred_element_type=jnp.float32) \
        + carry[...]
    e_out = jnp.zeros((MIX_ROWS, LANES), jnp.float32)
    g_out = jnp.zeros((MIX_ROWS, LANES), jnp.float32)
    r_out = jnp.zeros((MIX_ROWS, LANES), jnp.float32)
    for k in range(TOP_K):
        rank_k = jnp.sum(jnp.where(lane_f == idxs[k], rank_all, 0.0), axis=-1, keepdims=True)
        e_out = jnp.where(lane == k, idxs[k], e_out)
        g_out = jnp.where(lane == k, probs[k] / den, g_out)
        r_out = jnp.where(lane == k, rank_k, r_out)
    e_ref[...] = e_out.astype(jnp.int32)
    g_ref[...] = g_out
    r_ref[...] = r_out.astype(jnp.int32)
    carry[...] = carry[...] + jnp.sum(sel, axis=0, keepdims=True)
    cnt_ref[...] = jnp.broadcast_to(carry[...], cnt_ref.shape).astype(jnp.int32)


def _mix_ln_router(attn, y_f, y_b, proj, x, attn_norm_w, ssd_norm_w, w_out, ln1_w, ln1_b,
                   router_w, router_b):
    n, d = x.shape
    z_blk = (3 * ATTN_WIDTH) // SSD_WIDTH
    rw = jnp.pad(router_w, ((0, 0), (0, LANES - N_EXPERTS)))
    rw_hi = rw.astype(jnp.bfloat16)
    rw_lo = (rw - rw_hi.astype(jnp.float32)).astype(jnp.bfloat16)
    rb = jnp.pad(router_b, (0, LANES - N_EXPERTS)).reshape(1, LANES)
    tri = (jnp.arange(MIX_ROWS)[None, :] < jnp.arange(MIX_ROWS)[:, None]).astype(jnp.bfloat16)
    half = pl.BlockSpec((MIX_ROWS, ATTN_WIDTH), lambda i: (i, 0))
    vec = lambda width: pl.BlockSpec((1, width), lambda i: (0, 0))
    full = lambda r, c: pl.BlockSpec((r, c), lambda i: (0, 0))
    rows128 = pl.BlockSpec((MIX_ROWS, LANES), lambda i: (i, 0))
    return pl.pallas_call(
        _mix_kernel,
        grid=(n // MIX_ROWS,),
        in_specs=[half, half, half,
                  pl.BlockSpec((MIX_ROWS, SSD_WIDTH), lambda i: (i, z_blk)),
                  pl.BlockSpec((MIX_ROWS, d), lambda i: (i, 0)),
                  vec(ATTN_WIDTH), vec(SSD_WIDTH), full(d, d), vec(d), vec(d),
                  full(d, LANES), full(d, LANES), vec(LANES), full(MIX_ROWS, MIX_ROWS)],
        out_specs=[pl.BlockSpec((MIX_ROWS * SLAB, LANES), lambda i: (i, 0)),
                   pl.BlockSpec((MIX_ROWS * PACK, LANES), lambda i: (i, 0)),
                   rows128, rows128, rows128,
                   pl.BlockSpec((SLAB, LANES), lambda i: (0, 0))],
        out_shape=[jax.ShapeDtypeStruct((n * SLAB, LANES), jnp.float32),
                   jax.ShapeDtypeStruct((n * PACK, LANES), jnp.uint32),
                   jax.ShapeDtypeStruct((n, LANES), jnp.int32),
                   jax.ShapeDtypeStruct((n, LANES), jnp.float32),
                   jax.ShapeDtypeStruct((n, LANES), jnp.int32),
                   jax.ShapeDtypeStruct((SLAB, LANES), jnp.int32)],
        scratch_shapes=[pltpu.VMEM((1, LANES), jnp.float32)],
        compiler_params=pltpu.CompilerParams(
            dimension_semantics=("arbitrary",),
            vmem_limit_bytes=48 * 1024 * 1024),
        name="mix_ln_router",
    )(attn, y_f, y_b, proj, x, attn_norm_w.reshape(1, ATTN_WIDTH), ssd_norm_w.reshape(1, SSD_WIDTH),
      w_out.astype(jnp.bfloat16), ln1_w.reshape(1, d), ln1_b.reshape(1, d), rw_hi, rw_lo, rb, tri)


def _slab_of(row):
    return pl.ds(pl.multiple_of(row * PACK, PACK), PACK)


def _dispatch_kernel(last_ref, has_ref, nt_ref, pos_ref, h_ref, xs_hbm, zbuf, sem, zsem):
    i = pl.program_id(0)
    t_max = xs_hbm.shape[0] // (MOE_TILE * PACK)

    @pl.when(i == 0)
    def _zero_fill():
        zbuf[...] = jnp.zeros_like(zbuf)

        def fill(tile):
            return pltpu.make_async_copy(
                zbuf, xs_hbm.at[pl.ds(pl.multiple_of(tile * (MOE_TILE * PACK), MOE_TILE * PACK),
                                      MOE_TILE * PACK), :], zsem)

        for e in range(N_EXPERTS):
            @pl.when(has_ref[e] > 0)
            def _(e=e):
                fill(last_ref[e]).start()
        for j in range(N_EXPERTS + 1):
            @pl.when(nt_ref[0] + j < t_max)
            def _(j=j):
                fill(nt_ref[0] + j).start()
        for e in range(N_EXPERTS):
            @pl.when(has_ref[e] > 0)
            def _(e=e):
                fill(last_ref[e]).wait()
        for j in range(N_EXPERTS + 1):
            @pl.when(nt_ref[0] + j < t_max)
            def _(j=j):
                fill(nt_ref[0] + j).wait()

    def copy(k, r):
        return pltpu.make_async_copy(h_ref.at[_slab_of(r), :],
                                     xs_hbm.at[_slab_of(pos_ref[0, 0, k * MIX_ROWS + r]), :], sem)

    def start(r, c):
        for k in range(TOP_K):
            copy(k, r).start(priority=k % 2)
        return c

    def wait(r, c):
        for k in range(TOP_K):
            copy(k, r).wait()
        return c

    lax.fori_loop(0, MIX_ROWS, start, 0, unroll=4)
    lax.fori_loop(0, MIX_ROWS, wait, 0, unroll=4)


def _dispatch(h1_packed, pos_tab, last_tile, has_tile, n_tiles, n_rows):
    nb = pos_tab.shape[0]
    grid_spec = pltpu.PrefetchScalarGridSpec(
        num_scalar_prefetch=3,
        grid=(nb,),
        in_specs=[pl.BlockSpec((1, 1, TOP_K * MIX_ROWS), lambda i, la, ha, nt: (i, 0, 0),
                               memory_space=pltpu.SMEM),
                  pl.BlockSpec((MIX_ROWS * PACK, LANES), lambda i, la, ha, nt: (i, 0))],
        out_specs=pl.BlockSpec(memory_space=pl.ANY),
        scratch_shapes=[pltpu.VMEM((MOE_TILE * PACK, LANES), jnp.uint32),
                        pltpu.SemaphoreType.DMA(()),
                        pltpu.SemaphoreType.DMA(())],
    )
    return pl.pallas_call(
        _dispatch_kernel,
        grid_spec=grid_spec,
        out_shape=jax.ShapeDtypeStruct((n_rows * PACK, LANES), jnp.uint32),
        compiler_params=pltpu.CompilerParams(
            dimension_semantics=("arbitrary",),
            vmem_limit_bytes=32 * 1024 * 1024),
        name="dispatch",
    )(last_tile, has_tile, n_tiles, pos_tab, h1_packed)


def _moe_kernel(te_ref, nt_ref, dst_prv_ref,
                x_ref, wg_ref, wu_ref, wd_ref, bg_ref, bu_ref, bd_ref,
                y_hbm,
                ybuf, xb_s, act_s, wg_bf, wu_bf, wd_bf, ssem):
    i = pl.program_id(0)
    n_tiles = nt_ref[0]
    slot = i % 2
    other = 1 - slot
    n_pairs = y_hbm.shape[0] // PACK - 2 * MOE_TILE
    bf16 = jnp.bfloat16

    def scatter_copy(dst, row, s):
        return pltpu.make_async_copy(ybuf.at[s, _slab_of(row), :],
                                     y_hbm.at[_slab_of(dst), :], ssem.at[s])

    def start_scatter_prev(r, prio=0):
        dst = jnp.where(i == 0, n_pairs + MOE_TILE + r, dst_prv_ref[0, 0, r])
        scatter_copy(dst, r, other).start(priority=prio)

    def wait_scatter(s):
        def body(r, c):
            scatter_copy(0, r, s).wait()
            return c
        lax.fori_loop(0, MOE_TILE, body, 0, unroll=8)

    @pl.when(i == 0)
    def _first():
        ybuf[...] = jnp.zeros_like(ybuf)
        for half in range(2):
            cp = pltpu.make_async_copy(
                ybuf.at[0],
                y_hbm.at[pl.ds((n_pairs + half * MOE_TILE) * PACK, MOE_TILE * PACK), :],
                ssem.at[half])
            cp.start()
            cp.wait()

    @pl.when((i >= 1) & (i <= n_tiles))
    def _arrive():
        wait_scatter(slot)

    @pl.when(i < n_tiles)
    def _compute():
        e = te_ref[i]
        prev_e = te_ref[jnp.maximum(i - 1, 0)]

        @pl.when((i == 0) | (e != prev_e))
        def _():
            wg_bf[...] = wg_ref[0].astype(bf16)
            wu_bf[...] = wu_ref[0].astype(bf16)
            wd_bf[...] = wd_ref[0].astype(bf16)

        for c in range(PACK):
            lo, hi = _unpack_pair(x_ref[pl.ds(c, MOE_TILE, stride=PACK), :])
            xb_s[:, (2 * c) * LANES:(2 * c + 1) * LANES] = lo.astype(bf16)
            xb_s[:, (2 * c + 1) * LANES:(2 * c + 2) * LANES] = hi.astype(bf16)

        batch = MOE_TILE // (2 * (D_MODEL // FFN_CHUNK))
        for c in range(D_MODEL // FFN_CHUNK):
            cols = slice(c * FFN_CHUNK, (c + 1) * FFN_CHUNK)
            g = jnp.dot(xb_s[...], wg_bf[:, cols], preferred_element_type=jnp.float32)
            g = g + bg_ref[pl.ds(e, 1), cols]
            for r in range(2 * c * batch, (2 * c + 1) * batch):
                start_scatter_prev(r, r % 2)
            u = jnp.dot(xb_s[...], wu_bf[:, cols], preferred_element_type=jnp.float32)
            u = u + bu_ref[pl.ds(e, 1), cols]
            for r in range((2 * c + 1) * batch, (2 * c + 2) * batch):
                start_scatter_prev(r, r % 2)
            g = jnp.minimum(g, SWIGLU_LIMIT)
            u = jnp.clip(u, -SWIGLU_LIMIT, SWIGLU_LIMIT)
            act_s[:, cols] = ((u + 1.0) * g * jax.nn.sigmoid(SWIGLU_ALPHA * g)).astype(bf16)

        for c in range(D_MODEL // FFN_CHUNK):
            cols = slice(c * FFN_CHUNK, (c + 1) * FFN_CHUNK)
            y = jnp.dot(act_s[...], wd_bf[:, cols], preferred_element_type=jnp.float32)
            y = y + bd_ref[pl.ds(e, 1), cols]
            ybuf[slot, pl.ds(c, MOE_TILE, stride=PACK), :] = _pack_pair(y[:, :LANES], y[:, LANES:])

    @pl.when(i == n_tiles)
    def _drain():
        def body(r, c):
            start_scatter_prev(r)
            return c
        lax.fori_loop(0, MOE_TILE, body, 0, unroll=8)
        wait_scatter(other)


def _moe_ffn(xs, tile_e, n_tiles, dst_row, w_gate, b_gate, w_up, b_up, w_down, b_down,
             n_out_rows):
    t_max = tile_e.shape[0]
    d = D_MODEL
    w_spec = pl.BlockSpec((1, d, d), lambda i, te, nt: (te[i], 0, 0))
    b_spec = pl.BlockSpec((N_EXPERTS, d), lambda i, te, nt: (0, 0))
    grid_spec = pltpu.PrefetchScalarGridSpec(
        num_scalar_prefetch=2,
        grid=(t_max,),
        in_specs=[pl.BlockSpec((1, 1, MOE_TILE), lambda i, te, nt: (jnp.maximum(i - 1, 0), 0, 0),
                               memory_space=pltpu.SMEM),
                  pl.BlockSpec((MOE_TILE * PACK, LANES),
                               lambda i, te, nt: (jnp.minimum(i, nt[0] - 1), 0)),
                  w_spec, w_spec, w_spec, b_spec, b_spec, b_spec],
        out_specs=pl.BlockSpec(memory_space=pl.ANY),
        scratch_shapes=[
            pltpu.VMEM((2, MOE_TILE * PACK, LANES), jnp.uint32),
            pltpu.VMEM((MOE_TILE, d), jnp.bfloat16),
            pltpu.VMEM((MOE_TILE, d), jnp.bfloat16),
            pltpu.VMEM((d, d), jnp.bfloat16),
            pltpu.VMEM((d, d), jnp.bfloat16),
            pltpu.VMEM((d, d), jnp.bfloat16),
            pltpu.SemaphoreType.DMA((2,)),
        ],
    )
    return pl.pallas_call(
        _moe_kernel,
        grid_spec=grid_spec,
        out_shape=jax.ShapeDtypeStruct((n_out_rows * PACK, LANES), jnp.uint32),
        compiler_params=pltpu.CompilerParams(
            dimension_semantics=("arbitrary",),
            vmem_limit_bytes=V7X_VMEM_BYTES - 6 * 1024 * 1024),
        name="moe_ffn",
    )(tile_e, n_tiles, dst_row, xs, w_gate, w_up, w_down, b_gate, b_up, b_down)


def _combine_ln_kernel(y0_ref, y1_ref, y2_ref, y3_ref, h_ref, g_ref, w_ref, b_ref, o_ref):
    def plane_rows(ref):
        tiles = []
        for c in range(PACK):
            tiles.extend(_unpack_pair(ref[pl.ds(c, LN_ROWS, stride=PACK), :]))
        return jnp.concatenate(tiles, axis=-1)

    g = g_ref[...]
    ffn = (g[:, 0:1] * plane_rows(y0_ref) + g[:, 1:2] * plane_rows(y1_ref)
           + g[:, 2:3] * plane_rows(y2_ref) + g[:, 3:4] * plane_rows(y3_ref))
    v = DEEPNORM_ALPHA * _slab_rows(h_ref, LN_ROWS) + ffn
    mu = jnp.mean(v, axis=-1, keepdims=True)
    var = jnp.mean(jnp.square(v - mu), axis=-1, keepdims=True)
    o_ref[...] = (v - mu) * lax.rsqrt(var + NORM_EPS) * w_ref[...] + b_ref[...]


def _combine_ln(y_planes, h1_slab, gates, ln_w, ln_b):
    n = gates.shape[0]
    d = D_MODEL
    nb = n // LN_ROWS

    def plane(k):
        return pl.BlockSpec((LN_ROWS * PACK, LANES), lambda i, k=k: (k * nb + i, 0))

    vec = pl.BlockSpec((1, d), lambda i: (0, 0))
    return pl.pallas_call(
        _combine_ln_kernel,
        grid=(nb,),
        in_specs=[plane(0), plane(1), plane(2), plane(3),
                  pl.BlockSpec((LN_ROWS * SLAB, LANES), lambda i: (i, 0)),
                  pl.BlockSpec((LN_ROWS, LANES), lambda i: (i, 0)), vec, vec],
        out_specs=pl.BlockSpec((LN_ROWS, d), lambda i: (i, 0)),
        out_shape=jax.ShapeDtypeStruct((n, d), jnp.float32),
        compiler_params=pltpu.CompilerParams(
            dimension_semantics=("arbitrary",),
            vmem_limit_bytes=48 * 1024 * 1024),
        name="combine_ln",
    )(y_planes, y_planes, y_planes, y_planes, h1_slab, gates,
      ln_w.reshape(1, d), ln_b.reshape(1, d))


def _routing_tables(top_e, rank, counts, n_tok):
    n_pairs = TOP_K * n_tok
    t_max = n_pairs // MOE_TILE + N_EXPERTS + 1
    n_rows = t_max * MOE_TILE
    tiles_per_e = (counts + MOE_TILE - 1) // MOE_TILE
    tile_end = jnp.cumsum(tiles_per_e)
    tile_start = tile_end - tiles_per_e
    n_tiles = tile_end[-1]
    tile_ids = jnp.arange(t_max, dtype=jnp.int32)
    tile_e = jnp.sum((tile_end[None, :] <= jnp.minimum(tile_ids, n_tiles - 1)[:, None])
                     .astype(jnp.int32), axis=1)
    tile_e = jnp.minimum(tile_e, N_EXPERTS - 1).astype(jnp.int32)

    base = (tile_start * MOE_TILE).astype(jnp.int32)
    experts = jnp.arange(N_EXPERTS, dtype=jnp.int32)
    pos = jnp.sum(jnp.where(top_e[:, :, None] == experts[None, None, :], base[None, None, :], 0),
                  axis=-1) + rank
    pair_id = (jnp.arange(n_tok, dtype=jnp.int32)[:, None]
               + n_tok * jnp.arange(TOP_K, dtype=jnp.int32)[None, :])
    _, pairs_by_row = lax.sort_key_val(pos.reshape(-1), pair_id.reshape(-1))
    first = (jnp.cumsum(counts) - counts).astype(jnp.int32)
    lane = jnp.arange(MOE_TILE, dtype=jnp.int32)[None, :]
    row_in_e = (tile_ids - tile_start[tile_e])[:, None] * MOE_TILE + lane
    valid = (row_in_e < counts[tile_e][:, None]) & (tile_ids < n_tiles)[:, None]
    owner = pairs_by_row[jnp.clip(first[tile_e][:, None] + row_in_e, 0, n_pairs - 1)]
    spare = n_pairs + (tile_ids % 2)[:, None] * MOE_TILE + lane
    dst_row = jnp.where(valid, owner, spare).astype(jnp.int32).reshape(t_max, 1, MOE_TILE)

    nb = n_tok // MIX_ROWS
    pos_tab = pos.reshape(nb, MIX_ROWS, TOP_K).transpose(0, 2, 1).reshape(nb, 1, TOP_K * MIX_ROWS)
    last_tile = jnp.maximum(tile_end - 1, 0).astype(jnp.int32)
    has_tile = (tiles_per_e > 0).astype(jnp.int32)
    return (tile_e, n_tiles.reshape(1).astype(jnp.int32), pos_tab, dst_row, last_tile, has_tile,
            n_rows)


def kernel(x, w_in, attn_norm_w, conv_w, conv_b, dt_bias_fwd, a_log_fwd, dt_bias_bwd,
           a_log_bwd, d_skip, ssd_norm_w, w_out, ln1_w, ln1_b, router_w, router_b,
           w_gate, b_gate, w_up, b_up, w_down, b_down, ln2_w, ln2_b):
    b, s, d = x.shape
    n = b * s
    xf = x.reshape(n, d)
    in_cols = w_in.shape[-1]
    in_cols_pad = -(-in_cols // LANES) * LANES

    w_in_b = jnp.pad(w_in[0], ((0, 0), (0, in_cols_pad - in_cols))).astype(jnp.bfloat16)
    proj = _rows_matmul(xf, w_in_b, "in_proj")
    proj3 = proj.reshape(b, s, in_cols_pad)
    attn = _attention(proj3)
    y_f, y_b = _ssd_pre_gate(proj3, conv_w[0], conv_b[0], dt_bias_fwd[0], a_log_fwd[0],
                             dt_bias_bwd[0], a_log_bwd[0], d_skip[0])
    h1_slab, h1_packed, top_e, gates, rank, counts = _mix_ln_router(
        attn.reshape(n, ATTN_WIDTH), y_f.reshape(n, SSD_WIDTH), y_b.reshape(n, SSD_WIDTH),
        proj, xf, attn_norm_w[0], ssd_norm_w[0], w_out[0], ln1_w[0], ln1_b[0],
        router_w[0], router_b[0])

    tile_e, n_tiles, pos_tab, dst_row, last_tile, has_tile, n_rows = _routing_tables(
        top_e[:, :TOP_K], rank[:, :TOP_K], counts[0, :N_EXPERTS], n)
    xs = _dispatch(h1_packed, pos_tab, last_tile, has_tile, n_tiles, n_rows)
    n_out_rows = TOP_K * n + 2 * MOE_TILE
    y_planes = _moe_ffn(xs, tile_e, n_tiles, dst_row,
                        w_gate[0], b_gate[0], w_up[0], b_up[0], w_down[0], b_down[0],
                        n_out_rows)
    out = _combine_ln(y_planes, h1_slab, gates, ln2_w[0], ln2_b[0])
    return out.reshape(b, s, d)
```

```python
import functools

import jax
import jax.numpy as jnp
from jax import lax
from jax.experimental import pallas as pl
from jax.experimental.pallas import tpu as pltpu

D_MODEL = 1024
HEAD_DIM = 64
ATTN_HEADS = 8
ATTN_WIDTH = ATTN_HEADS * HEAD_DIM
DILATED_PATTERNS = ((128, 1), (512, 4), (2048, 16))
ROPE_THETA = 10000.0
SSD_HEADS = 8
SSD_HEAD_DIM = 64
SSD_WIDTH = SSD_HEADS * SSD_HEAD_DIM
SSD_GROUPS = 2
SSD_STATE = 128
SSD_CONV = 5
SSD_CHUNK = 128
SSD_CONV_CH = SSD_WIDTH + 2 * SSD_GROUPS * SSD_STATE
N_EXPERTS = 32
TOP_K = 4
SWIGLU_ALPHA = 1.702
SWIGLU_LIMIT = 7.0
DEPTH = 1
DEEPNORM_ALPHA = (2.0 * DEPTH) ** 0.25
NORM_EPS = 1e-5

V7X_VMEM_BYTES = 64 * 1024 * 1024
LANES = 128

MATMUL_ROWS = 512
MIX_ROWS = 1024
MOE_TILE = 512
FFN_CHUNK = 1024
LN_ROWS = 1024
SLAB = D_MODEL // LANES
PACK = SLAB // 2


def _slab_rows(ref, rows):
    return jnp.concatenate(
        [ref[pl.ds(c, rows, stride=SLAB), :] for c in range(SLAB)], axis=-1)


def _pack_pair(lo, hi):
    lo_bits = lax.bitcast_convert_type(lo.astype(jnp.bfloat16).astype(jnp.float32), jnp.uint32)
    hi_bits = lax.bitcast_convert_type(hi.astype(jnp.bfloat16).astype(jnp.float32), jnp.uint32)
    return (lo_bits >> 16) | (hi_bits & jnp.uint32(0xFFFF0000))


def _unpack_pair(word):
    lo = lax.bitcast_convert_type(word << 16, jnp.float32)
    hi = lax.bitcast_convert_type(word & jnp.uint32(0xFFFF0000), jnp.float32)
    return lo, hi


def _store_packed(ref, rows, value):
    for c in range(PACK):
        lo = value[:, (2 * c) * LANES:(2 * c + 1) * LANES]
        hi = value[:, (2 * c + 1) * LANES:(2 * c + 2) * LANES]
        ref[pl.ds(c, rows, stride=PACK), :] = _pack_pair(lo, hi)


def _rows_matmul_kernel(x_ref, w_ref, o_ref):
    o_ref[...] = jnp.dot(x_ref[...].astype(jnp.bfloat16), w_ref[...],
                         preferred_element_type=jnp.float32)


def _rows_matmul(x, w_bf16, name):
    n, k = x.shape
    _, m = w_bf16.shape
    assert n % MATMUL_ROWS == 0 and m % LANES == 0
    return pl.pallas_call(
        _rows_matmul_kernel,
        grid=(n // MATMUL_ROWS,),
        in_specs=[pl.BlockSpec((MATMUL_ROWS, k), lambda i: (i, 0)),
                  pl.BlockSpec((k, m), lambda i: (0, 0))],
        out_specs=pl.BlockSpec((MATMUL_ROWS, m), lambda i: (i, 0)),
        out_shape=jax.ShapeDtypeStruct((n, m), jnp.float32),
        compiler_params=pltpu.CompilerParams(
            dimension_semantics=("arbitrary",),
            vmem_limit_bytes=48 * 1024 * 1024),
        name=name,
    )(x, w_bf16)


ATTN_BLOCK = 128
ATTN_WINDOW = 256
ATTN_HALF = 64
ATTN_STAGE_ROWS = 256


def _rope_tables(s):
    pos = jnp.arange(s, dtype=jnp.float32)
    inv_freq = ROPE_THETA ** (-jnp.arange(0, HEAD_DIM, 2, dtype=jnp.float32) / HEAD_DIM)
    ang = pos[:, None] * inv_freq[None, :]
    return jnp.cos(ang), jnp.sin(ang)


def _strided_rows(start, size, stride):
    return pl.ds(start, size) if stride == 1 else pl.ds(start, size, stride=stride)


def _attention_kernel(q_ref, k_ref, v_ref, cos_ref, sin_lo_ref, sin_hi_ref, o_ref,
                      qr, kr, q0s, q1s, ks, vs, acc_o, acc_m, acc_l):
    s_len = q_ref.shape[0]
    head0 = lax.broadcasted_iota(jnp.int32, (1, LANES), 1) < HEAD_DIM
    q_i = lax.broadcasted_iota(jnp.int32, (ATTN_BLOCK, ATTN_WINDOW), 0)
    k_j = lax.broadcasted_iota(jnp.int32, (ATTN_BLOCK, ATTN_WINDOW), 1)
    rel = q_i - k_j
    last = len(DILATED_PATTERNS) - 1

    def rotate(j, carry):
        rows = pl.ds(pl.multiple_of(j * ATTN_STAGE_ROWS, ATTN_STAGE_ROWS), ATTN_STAGE_ROWS)
        cos = cos_ref[rows, :]
        sin_lo = sin_lo_ref[rows, :]
        sin_hi = sin_hi_ref[rows, :]

        def rope(t):
            return (t * cos + pltpu.roll(t, LANES - HEAD_DIM // 2, 1) * sin_lo
                    + pltpu.roll(t, HEAD_DIM // 2, 1) * sin_hi)

        qr[rows, :] = rope(q_ref[rows, :]) * (HEAD_DIM ** -0.5)
        kr[rows, :] = rope(k_ref[rows, :])
        return carry

    lax.fori_loop(0, s_len // ATTN_STAGE_ROWS, rotate, 0, unroll=16)

    for pi, (window, dil) in enumerate(DILATED_PATTERNS):
        assert window // (2 * dil) == ATTN_HALF
        row_len = s_len // dil
        assert row_len % ATTN_WINDOW == 0 and row_len % ATTN_STAGE_ROWS == 0

        def stage(j, carry, dil=dil, row_len=row_len):
            f0 = pl.multiple_of(j * ATTN_STAGE_ROWS, ATTN_STAGE_ROWS)
            r = f0 // row_len
            src = _strided_rows((f0 - r * row_len) * dil + r, ATTN_STAGE_ROWS, dil)
            dst = pl.ds(f0, ATTN_STAGE_ROWS)
            qv = qr[src, :]
            q0s[dst, :] = jnp.where(head0, qv, 0.0).astype(jnp.bfloat16)
            q1s[dst, :] = jnp.where(head0, 0.0, qv).astype(jnp.bfloat16)
            ks[dst, :] = kr[src, :].astype(jnp.bfloat16)
            vs[dst, :] = v_ref[src, :].astype(jnp.bfloat16)
            return carry

        lax.fori_loop(0, s_len // ATTN_STAGE_ROWS, stage, 0, unroll=16)

        def block(bi, carry, pi=pi, dil=dil, row_len=row_len):
            f0 = pl.multiple_of(bi * ATTN_BLOCK, ATTN_BLOCK)
            r = f0 // row_len
            row_start = r * row_len
            win = jnp.clip(f0 - ATTN_HALF, row_start, row_start + row_len - ATTN_WINDOW)
            win = pl.multiple_of(win, ATTN_HALF)
            valid = jnp.abs(rel + (f0 - win)) <= ATTN_HALF
            kw = ks[pl.ds(win, ATTN_WINDOW), :]
            vw = vs[pl.ds(win, ATTN_WINDOW), :]
            parts = []
            for qs_ref in (q0s, q1s):
                qb = qs_ref[pl.ds(f0, ATTN_BLOCK), :]
                sc = lax.dot_general(qb, kw, (((1,), (1,)), ((), ())),
                                     preferred_element_type=jnp.float32)
                sc = jnp.where(valid, sc, -jnp.inf)
                m = jnp.max(sc, axis=-1, keepdims=True)
                p = jnp.exp(sc - m)
                den = jnp.sum(p, axis=-1, keepdims=True)
                num = jnp.dot(p.astype(jnp.bfloat16), vw, preferred_element_type=jnp.float32)
                parts.append((num, m, den))
            num = jnp.where(head0, parts[0][0], parts[1][0])
            m = jnp.where(head0, parts[0][1], parts[1][1])
            den = jnp.where(head0, parts[0][2], parts[1][2])
            rows = _strided_rows((f0 - row_start) * dil + r, ATTN_BLOCK, dil)
            if pi == 0:
                acc_o[rows, :] = num
                acc_m[rows, :] = m
                acc_l[rows, :] = den
            else:
                m_old = acc_m[rows, :]
                m_new = jnp.maximum(m_old, m)
                w_old = jnp.exp(m_old - m_new)
                w_cur = jnp.exp(m - m_new)
                num = acc_o[rows, :] * w_old + num * w_cur
                den = acc_l[rows, :] * w_old + den * w_cur
                if pi == last:
                    o_ref[rows, :] = num / den
                else:
                    acc_o[rows, :] = num
                    acc_m[rows, :] = m_new
                    acc_l[rows, :] = den
            return carry

        lax.fori_loop(0, s_len // ATTN_BLOCK, block, 0, unroll=32)


def _attention(proj):
    b, s, _ = proj.shape
    cos, sin = _rope_tables(s)
    reps = LANES // (HEAD_DIM // 2)
    first_half = (jnp.arange(LANES) % HEAD_DIM) < HEAD_DIM // 2
    cos_t = jnp.tile(cos, (1, reps))
    sin_t = jnp.tile(sin, (1, reps))
    sin_lo = jnp.where(first_half, -sin_t, 0.0)
    sin_hi = jnp.where(first_half, 0.0, sin_t)
    pairs = ATTN_WIDTH // LANES

    def cols(off):
        return pl.BlockSpec((None, s, LANES), lambda bi, hp, off=off: (bi, 0, off + hp))

    table = pl.BlockSpec((s, LANES), lambda bi, hp: (0, 0))
    return pl.pallas_call(
        _attention_kernel,
        grid=(b, pairs),
        in_specs=[cols(0), cols(pairs), cols(2 * pairs), table, table, table],
        out_specs=pl.BlockSpec((None, s, LANES), lambda bi, hp: (bi, 0, hp)),
        out_shape=jax.ShapeDtypeStruct((b, s, ATTN_WIDTH), jnp.float32),
        scratch_shapes=[pltpu.VMEM((s, LANES), jnp.float32)] * 2
        + [pltpu.VMEM((s, LANES), jnp.bfloat16)] * 4
        + [pltpu.VMEM((s, LANES), jnp.float32)] * 3,
        compiler_params=pltpu.CompilerParams(
            dimension_semantics=("arbitrary", "arbitrary"),
            vmem_limit_bytes=56 * 1024 * 1024),
        name="dilated_attention",
    )(proj, proj, proj, cos_t, sin_lo, sin_hi)


SSD_HALO = 8
SSD_GROUP_HEADS = SSD_HEADS // SSD_GROUPS
SSD_GROUP_WIDTH = SSD_GROUP_HEADS * SSD_HEAD_DIM


def _softplus(v):
    return jnp.maximum(v, 0.0) + jnp.log(1.0 + jnp.exp(-jnp.abs(v)))


def _expand_heads(mat, first_head):
    t = mat.shape[0]
    low = lax.broadcasted_iota(jnp.int32, (1, LANES), 1) < SSD_HEAD_DIM
    tiles = []
    for j in range(0, SSD_GROUP_HEADS, 2):
        a = jnp.broadcast_to(mat[:, first_head + j:first_head + j + 1], (t, LANES))
        b = jnp.broadcast_to(mat[:, first_head + j + 1:first_head + j + 2], (t, LANES))
        tiles.append(jnp.where(low, a, b))
    return jnp.concatenate(tiles, axis=-1)


def _ssd_kernel(xbc_ref, prev_ref, next_ref, dt_ref, cw_ref, cb_ref, dtb_ref, alog_ref, dsk_ref,
                y_ref, ext, cv, state, *, reverse):
    i = pl.program_id(1)
    nc = pl.num_programs(1)
    c = (nc - 1 - i) if reverse else i
    t = SSD_CHUNK
    bf16 = jnp.bfloat16

    @pl.when(i == 0)
    def _():
        state[...] = jnp.zeros_like(state)

    ext[0:SSD_HALO, :] = jnp.where(c > 0, prev_ref[...], 0.0)
    ext[SSD_HALO:SSD_HALO + t, :] = xbc_ref[...]
    ext[SSD_HALO + t:, :] = jnp.where(c < nc - 1, next_ref[...], 0.0)
    for j in range(SSD_CONV_CH // LANES):
        cols = slice(j * LANES, (j + 1) * LANES)
        acc = jnp.broadcast_to(cb_ref[:, cols], (t, LANES))
        for k in range(SSD_CONV):
            acc = acc + cw_ref[k:k + 1, cols] * ext[pl.ds(SSD_HALO - SSD_CONV // 2 + k, t), cols]
        cv[:, cols] = acc * jax.nn.sigmoid(acc)

    dt = _softplus(dt_ref[...] + dtb_ref[...])
    a = dt * (-jnp.exp(alog_ref[...]))
    row = lax.broadcasted_iota(jnp.int32, (t, LANES), 0)
    lane = lax.broadcasted_iota(jnp.int32, (t, LANES), 1)
    cs = a
    shift = 1
    while shift < t:
        cs = cs + jnp.where(row >= shift, pltpu.roll(cs, shift, 0), 0.0)
        shift *= 2
    total = cs[t - 1:t, :]
    e = jnp.where(lane < SSD_HEADS, cs, total - cs + a)
    off = SSD_HEADS if reverse else 0
    edge = e[0:1, :] if reverse else e[t - 1:t, :]
    scale_y = jnp.exp(e)
    scale_s = jnp.exp(edge - e) * dt

    x_off = 0
    b_off = SSD_WIDTH
    c_off = SSD_WIDTH + SSD_GROUPS * SSD_STATE
    low = lax.broadcasted_iota(jnp.int32, (1, LANES), 1) < SSD_HEAD_DIM

    if not reverse:
        e_t = e.T
        dt_t = dt.T
        li = lax.broadcasted_iota(jnp.int32, (t, t), 0)
        si = lax.broadcasted_iota(jnp.int32, (t, t), 1)
        lower = si <= li
        upper = si >= li

    for g in range(SSD_GROUPS):
        bg = cv[:, b_off + g * SSD_STATE:b_off + (g + 1) * SSD_STATE].astype(bf16)
        cg = cv[:, c_off + g * SSD_STATE:c_off + (g + 1) * SSD_STATE].astype(bf16)
        xg = cv[:, x_off + g * SSD_GROUP_WIDTH:x_off + (g + 1) * SSD_GROUP_WIDTH]
        h0 = g * SSD_GROUP_HEADS
        st = state[g]
        y = jnp.dot(cg, st.astype(bf16), preferred_element_type=jnp.float32) \
            * _expand_heads(scale_y, off + h0)
        xw = (xg * _expand_heads(scale_s, off + h0)).astype(bf16)
        new = lax.dot_general(bg, xw, (((0,), (0,)), ((), ())),
                              preferred_element_type=jnp.float32)
        decay = _expand_heads(jnp.broadcast_to(jnp.exp(edge), (SSD_HALO, LANES)), off + h0)
        state[g] = st * decay[0:1, :] + new

        if not reverse:
            cbm = lax.dot_general(cg, bg, (((1,), (1,)), ((), ())),
                                  preferred_element_type=jnp.float32)
            pieces = []
            for j in range(0, SSD_GROUP_HEADS, 2):
                xp = xg[:, j * SSD_HEAD_DIM:(j + 2) * SSD_HEAD_DIM]
                yp = None
                for hh, keep in ((j, low), (j + 1, jnp.logical_not(low))):
                    h = h0 + hh
                    hb = SSD_HEADS + h
                    lf = jnp.exp(jnp.where(lower, e[:, h:h + 1] - e_t[h:h + 1, :], -jnp.inf))
                    lb = jnp.exp(jnp.where(upper, e[:, hb:hb + 1] - e_t[hb:hb + 1, :], -jnp.inf))
                    mh = cbm * (lf * dt_t[h:h + 1, :] + lb * dt_t[hb:hb + 1, :])
                    part = jnp.dot(mh.astype(bf16), jnp.where(keep, xp, 0.0).astype(bf16),
                                   preferred_element_type=jnp.float32)
                    yp = part if yp is None else yp + part
                pieces.append(yp)
            gcols = slice(g * SSD_GROUP_WIDTH, (g + 1) * SSD_GROUP_WIDTH)
            y = y + jnp.concatenate(pieces, axis=-1) + dsk_ref[:, gcols] * xg
        y_ref[:, g * SSD_GROUP_WIDTH:(g + 1) * SSD_GROUP_WIDTH] = y


def _ssd_scan(proj, conv_w, conv_b, dt_bias, a_log, d_skip, reverse):
    b, s, _ = proj.shape
    nc = s // SSD_CHUNK
    z_end = 3 * ATTN_WIDTH + SSD_WIDTH
    xbc_blk = z_end // SSD_CONV_CH
    dt_blk = (z_end + SSD_CONV_CH) // LANES
    assert z_end % SSD_CONV_CH == 0 and (z_end + SSD_CONV_CH) % LANES == 0
    halo_per_chunk = SSD_CHUNK // SSD_HALO
    n_halo = s // SSD_HALO

    def chunk(i):
        return (nc - 1 - i) if reverse else i

    cw = jnp.pad(conv_w, ((0, SSD_HALO - SSD_CONV), (0, 0)))
    pad16 = lambda v: jnp.pad(v, (0, LANES - 2 * SSD_HEADS)).reshape(1, LANES)
    vec = lambda width: pl.BlockSpec((1, width), lambda bi, i: (0, 0))
    return pl.pallas_call(
        functools.partial(_ssd_kernel, reverse=reverse),
        grid=(b, nc),
        in_specs=[
            pl.BlockSpec((None, SSD_CHUNK, SSD_CONV_CH), lambda bi, i: (bi, chunk(i), xbc_blk)),
            pl.BlockSpec((None, SSD_HALO, SSD_CONV_CH),
                         lambda bi, i: (bi, jnp.maximum(chunk(i) * halo_per_chunk - 1, 0), xbc_blk)),
            pl.BlockSpec((None, SSD_HALO, SSD_CONV_CH),
                         lambda bi, i: (bi, jnp.minimum((chunk(i) + 1) * halo_per_chunk, n_halo - 1),
                                        xbc_blk)),
            pl.BlockSpec((None, SSD_CHUNK, LANES), lambda bi, i: (bi, chunk(i), dt_blk)),
            pl.BlockSpec((SSD_HALO, SSD_CONV_CH), lambda bi, i: (0, 0)),
            vec(SSD_CONV_CH), vec(LANES), vec(LANES), vec(SSD_WIDTH),
        ],
        out_specs=pl.BlockSpec((None, SSD_CHUNK, SSD_WIDTH), lambda bi, i: (bi, chunk(i), 0)),
        out_shape=jax.ShapeDtypeStruct((b, s, SSD_WIDTH), jnp.float32),
        scratch_shapes=[
            pltpu.VMEM((SSD_CHUNK + 2 * SSD_HALO, SSD_CONV_CH), jnp.float32),
            pltpu.VMEM((SSD_CHUNK, SSD_CONV_CH), jnp.float32),
            pltpu.VMEM((SSD_GROUPS, SSD_STATE, SSD_GROUP_WIDTH), jnp.float32),
        ],
        compiler_params=pltpu.CompilerParams(
            dimension_semantics=("arbitrary", "arbitrary"),
            vmem_limit_bytes=32 * 1024 * 1024),
        name="ssd_bwd" if reverse else "ssd_fwd",
    )(proj, proj, proj, proj, cw, conv_b.reshape(1, SSD_CONV_CH), pad16(dt_bias), pad16(a_log),
      jnp.repeat(d_skip, SSD_HEAD_DIM).reshape(1, SSD_WIDTH))


def _ssd_pre_gate(proj, conv_w, conv_b, dt_bias_fwd, a_log_fwd, dt_bias_bwd, a_log_bwd, d_skip):
    dt_bias = jnp.concatenate([dt_bias_fwd, dt_bias_bwd])
    a_log = jnp.concatenate([a_log_fwd, a_log_bwd])
    y_f = _ssd_scan(proj, conv_w, conv_b, dt_bias, a_log, d_skip, reverse=False)
    y_b = _ssd_scan(proj, conv_w, conv_b, dt_bias, a_log, d_skip, reverse=True)
    return y_f, y_b


def _mix_kernel(attn_ref, yf_ref, yb_ref, z_ref, x_ref, anw_ref, snw_ref, wout_ref,
                l1w_ref, l1b_ref, rw_hi_ref, rw_lo_ref, rb_ref, tri_ref,
                h_ref, hp_ref, e_ref, g_ref, r_ref, cnt_ref, carry):
    bf16 = jnp.bfloat16

    @pl.when(pl.program_id(0) == 0)
    def _():
        carry[...] = jnp.zeros_like(carry)

    a = attn_ref[...]
    a = a * lax.rsqrt(jnp.mean(a * a, axis=-1, keepdims=True) + NORM_EPS) * anw_ref[...]
    z = z_ref[...]
    y = (yf_ref[...] + yb_ref[...]) * (z * jax.nn.sigmoid(z))
    groups = []
    for g in range(SSD_GROUPS):
        yg = y[:, g * SSD_GROUP_WIDTH:(g + 1) * SSD_GROUP_WIDTH]
        groups.append(yg * lax.rsqrt(jnp.mean(yg * yg, axis=-1, keepdims=True) + NORM_EPS))
    y = jnp.concatenate(groups, axis=-1) * snw_ref[...]
    mix = (jnp.dot(a.astype(bf16), wout_ref[0:ATTN_WIDTH, :], preferred_element_type=jnp.float32)
           + jnp.dot(y.astype(bf16), wout_ref[ATTN_WIDTH:, :], preferred_element_type=jnp.float32))
    v = DEEPNORM_ALPHA * x_ref[...] + mix
    mu = jnp.mean(v, axis=-1, keepdims=True)
    var = jnp.mean(jnp.square(v - mu), axis=-1, keepdims=True)
    h1 = (v - mu) * lax.rsqrt(var + NORM_EPS) * l1w_ref[...] + l1b_ref[...]
    for c in range(SLAB):
        h_ref[pl.ds(c, MIX_ROWS, stride=SLAB), :] = h1[:, c * LANES:(c + 1) * LANES]
    _store_packed(hp_ref, MIX_ROWS, h1)

    h_hi = h1.astype(bf16)
    h_lo = (h1 - h_hi.astype(jnp.float32)).astype(bf16)
    logits = (jnp.dot(h_hi, rw_hi_ref[...], preferred_element_type=jnp.float32)
              + jnp.dot(h_lo, rw_hi_ref[...], preferred_element_type=jnp.float32)
              + jnp.dot(h_hi, rw_lo_ref[...], preferred_element_type=jnp.float32)
              + rb_ref[...])
    lane = lax.broadcasted_iota(jnp.int32, (MIX_ROWS, LANES), 1)
    lane_f = lane.astype(jnp.float32)
    cur = jnp.where(lane < N_EXPERTS, logits, -jnp.inf)
    vals, idxs = [], []
    for _ in range(TOP_K):
        m = jnp.max(cur, axis=-1, keepdims=True)
        idx = jnp.min(jnp.where(cur == m, lane_f, float(LANES)), axis=-1, keepdims=True)
        vals.append(m)
        idxs.append(idx)
        cur = jnp.where(lane_f == idx, -jnp.inf, cur)
    probs = [jnp.exp(val - vals[0]) for val in vals]
    den = probs[0]
    for p in probs[1:]:
        den = den + p

    sel = jnp.zeros((MIX_ROWS, LANES), jnp.float32)
    for k in range(TOP_K):
        sel = sel + jnp.where(lane_f == idxs[k], 1.0, 0.0)
    rank_all = jnp.dot(tri_ref[...], sel.astype(bf16), preferred_element_type=jnp.float32) \
        + carry[...]
    e_out = jnp.zeros((MIX_ROWS, LANES), jnp.float32)
    g_out = jnp.zeros((MIX_ROWS, LANES), jnp.float32)
    r_out = jnp.zeros((MIX_ROWS, LANES), jnp.float32)
    for k in range(TOP_K):
        rank_k = jnp.sum(jnp.where(lane_f == idxs[k], rank_all, 0.0), axis=-1, keepdims=True)
        e_out = jnp.where(lane == k, idxs[k], e_out)
        g_out = jnp.where(lane == k, probs[k] / den, g_out)
        r_out = jnp.where(lane == k, rank_k, r_out)
    e_ref[...] = e_out.astype(jnp.int32)
    g_ref[...] = g_out
    r_ref[...] = r_out.astype(jnp.int32)
    carry[...] = carry[...] + jnp.sum(sel, axis=0, keepdims=True)
    cnt_ref[...] = jnp.broadcast_to(carry[...], cnt_ref.shape).astype(jnp.int32)


def _mix_ln_router(attn, y_f, y_b, proj, x, attn_norm_w, ssd_norm_w, w_out, ln1_w, ln1_b,
                   router_w, router_b):
    n, d = x.shape
    z_blk = (3 * ATTN_WIDTH) // SSD_WIDTH
    rw = jnp.pad(router_w, ((0, 0), (0, LANES - N_EXPERTS)))
    rw_hi = rw.astype(jnp.bfloat16)
    rw_lo = (rw - rw_hi.astype(jnp.float32)).astype(jnp.bfloat16)
    rb = jnp.pad(router_b, (0, LANES - N_EXPERTS)).reshape(1, LANES)
    tri = (jnp.arange(MIX_ROWS)[None, :] < jnp.arange(MIX_ROWS)[:, None]).astype(jnp.bfloat16)
    half = pl.BlockSpec((MIX_ROWS, ATTN_WIDTH), lambda i: (i, 0))
    vec = lambda width: pl.BlockSpec((1, width), lambda i: (0, 0))
    full = lambda r, c: pl.BlockSpec((r, c), lambda i: (0, 0))
    rows128 = pl.BlockSpec((MIX_ROWS, LANES), lambda i: (i, 0))
    return pl.pallas_call(
        _mix_kernel,
        grid=(n // MIX_ROWS,),
        in_specs=[half, half, half,
                  pl.BlockSpec((MIX_ROWS, SSD_WIDTH), lambda i: (i, z_blk)),
                  pl.BlockSpec((MIX_ROWS, d), lambda i: (i, 0)),
                  vec(ATTN_WIDTH), vec(SSD_WIDTH), full(d, d), vec(d), vec(d),
                  full(d, LANES), full(d, LANES), vec(LANES), full(MIX_ROWS, MIX_ROWS)],
        out_specs=[pl.BlockSpec((MIX_ROWS * SLAB, LANES), lambda i: (i, 0)),
                   pl.BlockSpec((MIX_ROWS * PACK, LANES), lambda i: (i, 0)),
                   rows128, rows128, rows128,
                   pl.BlockSpec((SLAB, LANES), lambda i: (0, 0))],
        out_shape=[jax.ShapeDtypeStruct((n * SLAB, LANES), jnp.float32),
                   jax.ShapeDtypeStruct((n * PACK, LANES), jnp.uint32),
                   jax.ShapeDtypeStruct((n, LANES), jnp.int32),
                   jax.ShapeDtypeStruct((n, LANES), jnp.float32),
                   jax.ShapeDtypeStruct((n, LANES), jnp.int32),
                   jax.ShapeDtypeStruct((SLAB, LANES), jnp.int32)],
        scratch_shapes=[pltpu.VMEM((1, LANES), jnp.float32)],
        compiler_params=pltpu.CompilerParams(
            dimension_semantics=("arbitrary",),
            vmem_limit_bytes=48 * 1024 * 1024),
        name="mix_ln_router",
    )(attn, y_f, y_b, proj, x, attn_norm_w.reshape(1, ATTN_WIDTH), ssd_norm_w.reshape(1, SSD_WIDTH),
      w_out.astype(jnp.bfloat16), ln1_w.reshape(1, d), ln1_b.reshape(1, d), rw_hi, rw_lo, rb, tri)


def _slab_of(row):
    return pl.ds(pl.multiple_of(row * PACK, PACK), PACK)


def _dispatch_kernel(last_ref, has_ref, nt_ref, pos_ref, h_ref, xs_hbm, zbuf, sem, zsem):
    i = pl.program_id(0)
    t_max = xs_hbm.shape[0] // (MOE_TILE * PACK)

    @pl.when(i == 0)
    def _zero_fill():
        zbuf[...] = jnp.zeros_like(zbuf)

        def fill(tile):
            return pltpu.make_async_copy(
                zbuf, xs_hbm.at[pl.ds(pl.multiple_of(tile * (MOE_TILE * PACK), MOE_TILE * PACK),
                                      MOE_TILE * PACK), :], zsem)

        for e in range(N_EXPERTS):
            @pl.when(has_ref[e] > 0)
            def _(e=e):
                fill(last_ref[e]).start()
        for j in range(N_EXPERTS + 1):
            @pl.when(nt_ref[0] + j < t_max)
            def _(j=j):
                fill(nt_ref[0] + j).start()
        for e in range(N_EXPERTS):
            @pl.when(has_ref[e] > 0)
            def _(e=e):
                fill(last_ref[e]).wait()
        for j in range(N_EXPERTS + 1):
            @pl.when(nt_ref[0] + j < t_max)
            def _(j=j):
                fill(nt_ref[0] + j).wait()

    def copy(k, r):
        return pltpu.make_async_copy(h_ref.at[_slab_of(r), :],
                                     xs_hbm.at[_slab_of(pos_ref[0, 0, k * MIX_ROWS + r]), :], sem)

    def start(r, c):
        for k in range(TOP_K):
            copy(k, r).start(priority=k % 2)
        return c

    def wait(r, c):
        for k in range(TOP_K):
            copy(k, r).wait()
        return c

    lax.fori_loop(0, MIX_ROWS, start, 0, unroll=4)
    lax.fori_loop(0, MIX_ROWS, wait, 0, unroll=4)


def _dispatch(h1_packed, pos_tab, last_tile, has_tile, n_tiles, n_rows):
    nb = pos_tab.shape[0]
    grid_spec = pltpu.PrefetchScalarGridSpec(
        num_scalar_prefetch=3,
        grid=(nb,),
        in_specs=[pl.BlockSpec((1, 1, TOP_K * MIX_ROWS), lambda i, la, ha, nt: (i, 0, 0),
                               memory_space=pltpu.SMEM),
                  pl.BlockSpec((MIX_ROWS * PACK, LANES), lambda i, la, ha, nt: (i, 0))],
        out_specs=pl.BlockSpec(memory_space=pl.ANY),
        scratch_shapes=[pltpu.VMEM((MOE_TILE * PACK, LANES), jnp.uint32),
                        pltpu.SemaphoreType.DMA(()),
                        pltpu.SemaphoreType.DMA(())],
    )
    return pl.pallas_call(
        _dispatch_kernel,
        grid_spec=grid_spec,
        out_shape=jax.ShapeDtypeStruct((n_rows * PACK, LANES), jnp.uint32),
        compiler_params=pltpu.CompilerParams(
            dimension_semantics=("arbitrary",),
            vmem_limit_bytes=32 * 1024 * 1024),
        name="dispatch",
    )(last_tile, has_tile, n_tiles, pos_tab, h1_packed)


def _moe_kernel(te_ref, nt_ref, dst_prv_ref,
                x_ref, wg_ref, wu_ref, wd_ref, bg_ref, bu_ref, bd_ref,
                y_hbm,
                ybuf, xb_s, act_s, wgu_bf, wd_bf, ssem):
    i = pl.program_id(0)
    n_tiles = nt_ref[0]
    slot = i % 2
    other = 1 - slot
    n_pairs = y_hbm.shape[0] // PACK - 2 * MOE_TILE
    bf16 = jnp.bfloat16

    def scatter_copy(dst, row, s):
        return pltpu.make_async_copy(ybuf.at[s, _slab_of(row), :],
                                     y_hbm.at[_slab_of(dst), :], ssem.at[s])

    def start_scatter_prev(r, prio=0):
        dst = jnp.where(i == 0, n_pairs + MOE_TILE + r, dst_prv_ref[0, 0, r])
        scatter_copy(dst, r, other).start(priority=prio)

    def wait_scatter(s):
        def body(r, c):
            scatter_copy(0, r, s).wait()
            return c
        lax.fori_loop(0, MOE_TILE, body, 0, unroll=8)

    @pl.when(i == 0)
    def _first():
        ybuf[...] = jnp.zeros_like(ybuf)
        for half in range(2):
            cp = pltpu.make_async_copy(
                ybuf.at[0],
                y_hbm.at[pl.ds((n_pairs + half * MOE_TILE) * PACK, MOE_TILE * PACK), :],
                ssem.at[half])
            cp.start()
            cp.wait()

    @pl.when((i >= 1) & (i <= n_tiles))
    def _arrive():
        wait_scatter(slot)

    @pl.when(i < n_tiles)
    def _compute():
        e = te_ref[i]
        prev_e = te_ref[jnp.maximum(i - 1, 0)]

        @pl.when((i == 0) | (e != prev_e))
        def _():
            for c in range(D_MODEL // FFN_CHUNK):
                cols = slice(c * FFN_CHUNK, (c + 1) * FFN_CHUNK)
                wgu_bf[:, (2 * c) * FFN_CHUNK:(2 * c + 1) * FFN_CHUNK] = wg_ref[0, :, cols].astype(bf16)
                wgu_bf[:, (2 * c + 1) * FFN_CHUNK:(2 * c + 2) * FFN_CHUNK] = \
                    wu_ref[0, :, cols].astype(bf16)
            wd_bf[...] = wd_ref[0].astype(bf16)

        for c in range(PACK):
            lo, hi = _unpack_pair(x_ref[pl.ds(c, MOE_TILE, stride=PACK), :])
            xb_s[:, (2 * c) * LANES:(2 * c + 1) * LANES] = lo.astype(bf16)
            xb_s[:, (2 * c + 1) * LANES:(2 * c + 2) * LANES] = hi.astype(bf16)

        batch = MOE_TILE // (D_MODEL // FFN_CHUNK)
        for c in range(D_MODEL // FFN_CHUNK):
            cols = slice(c * FFN_CHUNK, (c + 1) * FFN_CHUNK)
            gu = jnp.dot(xb_s[...], wgu_bf[:, (2 * c) * FFN_CHUNK:(2 * c + 2) * FFN_CHUNK],
                         preferred_element_type=jnp.float32)
            g = gu[:, :FFN_CHUNK] + bg_ref[pl.ds(e, 1), cols]
            u = gu[:, FFN_CHUNK:] + bu_ref[pl.ds(e, 1), cols]
            for r in range(c * batch, (c + 1) * batch):
                start_scatter_prev(r, r % 2)
            g = jnp.minimum(g, SWIGLU_LIMIT)
            u = jnp.clip(u, -SWIGLU_LIMIT, SWIGLU_LIMIT)
            act_s[:, cols] = ((u + 1.0) * g * jax.nn.sigmoid(SWIGLU_ALPHA * g)).astype(bf16)

        for c in range(D_MODEL // FFN_CHUNK):
            cols = slice(c * FFN_CHUNK, (c + 1) * FFN_CHUNK)
            y = jnp.dot(act_s[...], wd_bf[:, cols], preferred_element_type=jnp.float32)
            y = y + bd_ref[pl.ds(e, 1), cols]
            for j in range(FFN_CHUNK // (2 * LANES)):
                ybuf[slot, pl.ds(c * (FFN_CHUNK // (2 * LANES)) + j, MOE_TILE, stride=PACK), :] = \
                    _pack_pair(y[:, (2 * j) * LANES:(2 * j + 1) * LANES],
                               y[:, (2 * j + 1) * LANES:(2 * j + 2) * LANES])

    @pl.when(i == n_tiles)
    def _drain():
        def body(r, c):
            start_scatter_prev(r)
            return c
        lax.fori_loop(0, MOE_TILE, body, 0, unroll=8)
        wait_scatter(other)


def _moe_ffn(xs, tile_e, n_tiles, dst_row, w_gate, b_gate, w_up, b_up, w_down, b_down,
             n_out_rows):
    t_max = tile_e.shape[0]
    d = D_MODEL
    w_spec = pl.BlockSpec((1, d, d), lambda i, te, nt: (te[i], 0, 0))
    b_spec = pl.BlockSpec((N_EXPERTS, d), lambda i, te, nt: (0, 0))
    grid_spec = pltpu.PrefetchScalarGridSpec(
        num_scalar_prefetch=2,
        grid=(t_max,),
        in_specs=[pl.BlockSpec((1, 1, MOE_TILE), lambda i, te, nt: (jnp.maximum(i - 1, 0), 0, 0),
                               memory_space=pltpu.SMEM),
                  pl.BlockSpec((MOE_TILE * PACK, LANES),
                               lambda i, te, nt: (jnp.minimum(i, nt[0] - 1), 0)),
                  w_spec, w_spec, w_spec, b_spec, b_spec, b_spec],
        out_specs=pl.BlockSpec(memory_space=pl.ANY),
        scratch_shapes=[
            pltpu.VMEM((2, MOE_TILE * PACK, LANES), jnp.uint32),
            pltpu.VMEM((MOE_TILE, d), jnp.bfloat16),
            pltpu.VMEM((MOE_TILE, d), jnp.bfloat16),
            pltpu.VMEM((d, 2 * d), jnp.bfloat16),
            pltpu.VMEM((d, d), jnp.bfloat16),
            pltpu.SemaphoreType.DMA((2,)),
        ],
    )
    return pl.pallas_call(
        _moe_kernel,
        grid_spec=grid_spec,
        out_shape=jax.ShapeDtypeStruct((n_out_rows * PACK, LANES), jnp.uint32),
        compiler_params=pltpu.CompilerParams(
            dimension_semantics=("arbitrary",),
            vmem_limit_bytes=V7X_VMEM_BYTES - 6 * 1024 * 1024),
        name="moe_ffn",
    )(tile_e, n_tiles, dst_row, xs, w_gate, w_up, w_down, b_gate, b_up, b_down)


def _combine_ln_kernel(y0_ref, y1_ref, y2_ref, y3_ref, h_ref, g_ref, w_ref, b_ref, o_ref):
    def plane_rows(ref):
        tiles = []
        for c in range(PACK):
            tiles.extend(_unpack_pair(ref[pl.ds(c, LN_ROWS, stride=PACK), :]))
        return jnp.concatenate(tiles, axis=-1)

    g = g_ref[...]
    ffn = (g[:, 0:1] * plane_rows(y0_ref) + g[:, 1:2] * plane_rows(y1_ref)
           + g[:, 2:3] * plane_rows(y2_ref) + g[:, 3:4] * plane_rows(y3_ref))
    v = DEEPNORM_ALPHA * _slab_rows(h_ref, LN_ROWS) + ffn
    mu = jnp.mean(v, axis=-1, keepdims=True)
    var = jnp.mean(jnp.square(v - mu), axis=-1, keepdims=True)
    o_ref[...] = (v - mu) * lax.rsqrt(var + NORM_EPS) * w_ref[...] + b_ref[...]


def _combine_ln(y_planes, h1_slab, gates, ln_w, ln_b):
    n = gates.shape[0]
    d = D_MODEL
    nb = n // LN_ROWS

    def plane(k):
        return pl.BlockSpec((LN_ROWS * PACK, LANES), lambda i, k=k: (k * nb + i, 0))

    vec = pl.BlockSpec((1, d), lambda i: (0, 0))
    return pl.pallas_call(
        _combine_ln_kernel,
        grid=(nb,),
        in_specs=[plane(0), plane(1), plane(2), plane(3),
                  pl.BlockSpec((LN_ROWS * SLAB, LANES), lambda i: (i, 0)),
                  pl.BlockSpec((LN_ROWS, LANES), lambda i: (i, 0)), vec, vec],
        out_specs=pl.BlockSpec((LN_ROWS, d), lambda i: (i, 0)),
        out_shape=jax.ShapeDtypeStruct((n, d), jnp.float32),
        compiler_params=pltpu.CompilerParams(
            dimension_semantics=("arbitrary",),
            vmem_limit_bytes=48 * 1024 * 1024),
        name="combine_ln",
    )(y_planes, y_planes, y_planes, y_planes, h1_slab, gates,
      ln_w.reshape(1, d), ln_b.reshape(1, d))


def _routing_tables(top_e, rank, counts, n_tok):
    n_pairs = TOP_K * n_tok
    t_max = n_pairs // MOE_TILE + N_EXPERTS + 1
    n_rows = t_max * MOE_TILE
    tiles_per_e = (counts + MOE_TILE - 1) // MOE_TILE
    tile_end = jnp.cumsum(tiles_per_e)
    tile_start = tile_end - tiles_per_e
    n_tiles = tile_end[-1]
    tile_ids = jnp.arange(t_max, dtype=jnp.int32)
    tile_e = jnp.sum((tile_end[None, :] <= jnp.minimum(tile_ids, n_tiles - 1)[:, None])
                     .astype(jnp.int32), axis=1)
    tile_e = jnp.minimum(tile_e, N_EXPERTS - 1).astype(jnp.int32)

    base = (tile_start * MOE_TILE).astype(jnp.int32)
    experts = jnp.arange(N_EXPERTS, dtype=jnp.int32)
    pos = jnp.sum(jnp.where(top_e[:, :, None] == experts[None, None, :], base[None, None, :], 0),
                  axis=-1) + rank
    pair_id = (jnp.arange(n_tok, dtype=jnp.int32)[:, None]
               + n_tok * jnp.arange(TOP_K, dtype=jnp.int32)[None, :])
    _, pairs_by_row = lax.sort_key_val(pos.reshape(-1), pair_id.reshape(-1))
    first = (jnp.cumsum(counts) - counts).astype(jnp.int32)
    lane = jnp.arange(MOE_TILE, dtype=jnp.int32)[None, :]
    row_in_e = (tile_ids - tile_start[tile_e])[:, None] * MOE_TILE + lane
    valid = (row_in_e < counts[tile_e][:, None]) & (tile_ids < n_tiles)[:, None]
    owner = pairs_by_row[jnp.clip(first[tile_e][:, None] + row_in_e, 0, n_pairs - 1)]
    spare = n_pairs + (tile_ids % 2)[:, None] * MOE_TILE + lane
    dst_row = jnp.where(valid, owner, spare).astype(jnp.int32).reshape(t_max, 1, MOE_TILE)

    nb = n_tok // MIX_ROWS
    pos_tab = pos.reshape(nb, MIX_ROWS, TOP_K).transpose(0, 2, 1).reshape(nb, 1, TOP_K * MIX_ROWS)
    last_tile = jnp.maximum(tile_end - 1, 0).astype(jnp.int32)
    has_tile = (tiles_per_e > 0).astype(jnp.int32)
    return (tile_e, n_tiles.reshape(1).astype(jnp.int32), pos_tab, dst_row, last_tile, has_tile,
            n_rows)


def kernel(x, w_in, attn_norm_w, conv_w, conv_b, dt_bias_fwd, a_log_fwd, dt_bias_bwd,
           a_log_bwd, d_skip, ssd_norm_w, w_out, ln1_w, ln1_b, router_w, router_b,
           w_gate, b_gate, w_up, b_up, w_down, b_down, ln2_w, ln2_b):
    b, s, d = x.shape
    n = b * s
    xf = x.reshape(n, d)
    in_cols = w_in.shape[-1]
    in_cols_pad = -(-in_cols // LANES) * LANES

    w_in_b = jnp.pad(w_in[0], ((0, 0), (0, in_cols_pad - in_cols))).astype(jnp.bfloat16)
    proj = _rows_matmul(xf, w_in_b, "in_proj")
    proj3 = proj.reshape(b, s, in_cols_pad)
    attn = _attention(proj3)
    y_f, y_b = _ssd_pre_gate(proj3, conv_w[0], conv_b[0], dt_bias_fwd[0], a_log_fwd[0],
                             dt_bias_bwd[0], a_log_bwd[0], d_skip[0])
    h1_slab, h1_packed, top_e, gates, rank, counts = _mix_ln_router(
        attn.reshape(n, ATTN_WIDTH), y_f.reshape(n, SSD_WIDTH), y_b.reshape(n, SSD_WIDTH),
        proj, xf, attn_norm_w[0], ssd_norm_w[0], w_out[0], ln1_w[0], ln1_b[0],
        router_w[0], router_b[0])

    tile_e, n_tiles, pos_tab, dst_row, last_tile, has_tile, n_rows = _routing_tables(
        top_e[:, :TOP_K], rank[:, :TOP_K], counts[0, :N_EXPERTS], n)
    xs = _dispatch(h1_packed, pos_tab, last_tile, has_tile, n_tiles, n_rows)
    n_out_rows = TOP_K * n + 2 * MOE_TILE
    y_planes = _moe_ffn(xs, tile_e, n_tiles, dst_row,
                        w_gate[0], b_gate[0], w_up[0], b_up[0], w_down[0], b_down[0],
                        n_out_rows)
    out = _combine_ln(y_planes, h1_slab, gates, ln2_w[0], ln2_b[0])
    return out.reshape(b, s, d)
```

```python
import functools

import jax
import jax.numpy as jnp
from jax import lax
from jax.experimental import pallas as pl
from jax.experimental.pallas import tpu as pltpu

D_MODEL = 1024
HEAD_DIM = 64
ATTN_HEADS = 8
ATTN_WIDTH = ATTN_HEADS * HEAD_DIM
DILATED_PATTERNS = ((128, 1), (512, 4), (2048, 16))
ROPE_THETA = 10000.0
SSD_HEADS = 8
SSD_HEAD_DIM = 64
SSD_WIDTH = SSD_HEADS * SSD_HEAD_DIM
SSD_GROUPS = 2
SSD_STATE = 128
SSD_CONV = 5
SSD_CHUNK = 128
SSD_CONV_CH = SSD_WIDTH + 2 * SSD_GROUPS * SSD_STATE
N_EXPERTS = 32
TOP_K = 4
SWIGLU_ALPHA = 1.702
SWIGLU_LIMIT = 7.0
DEPTH = 1
DEEPNORM_ALPHA = (2.0 * DEPTH) ** 0.25
NORM_EPS = 1e-5

V7X_VMEM_BYTES = 64 * 1024 * 1024
LANES = 128

MATMUL_ROWS = 512
MIX_ROWS = 1024
MOE_TILE = 512
FFN_CHUNK = 1024
LN_ROWS = 1024
SLAB = D_MODEL // LANES
PACK = SLAB // 2


def _slab_rows(ref, rows):
    return jnp.concatenate(
        [ref[pl.ds(c, rows, stride=SLAB), :] for c in range(SLAB)], axis=-1)


def _pack_pair(lo, hi):
    lo_bits = lax.bitcast_convert_type(lo.astype(jnp.bfloat16).astype(jnp.float32), jnp.uint32)
    hi_bits = lax.bitcast_convert_type(hi.astype(jnp.bfloat16).astype(jnp.float32), jnp.uint32)
    return (lo_bits >> 16) | (hi_bits & jnp.uint32(0xFFFF0000))


def _unpack_pair(word):
    lo = lax.bitcast_convert_type(word << 16, jnp.float32)
    hi = lax.bitcast_convert_type(word & jnp.uint32(0xFFFF0000), jnp.float32)
    return lo, hi


def _store_packed(ref, rows, value):
    for c in range(PACK):
        lo = value[:, (2 * c) * LANES:(2 * c + 1) * LANES]
        hi = value[:, (2 * c + 1) * LANES:(2 * c + 2) * LANES]
        ref[pl.ds(c, rows, stride=PACK), :] = _pack_pair(lo, hi)


def _rows_matmul_kernel(x_ref, w_ref, o_ref):
    o_ref[...] = jnp.dot(x_ref[...].astype(jnp.bfloat16), w_ref[...],
                         preferred_element_type=jnp.float32)


def _rows_matmul(x, w_bf16, name):
    n, k = x.shape
    _, m = w_bf16.shape
    assert n % MATMUL_ROWS == 0 and m % LANES == 0
    return pl.pallas_call(
        _rows_matmul_kernel,
        grid=(n // MATMUL_ROWS,),
        in_specs=[pl.BlockSpec((MATMUL_ROWS, k), lambda i: (i, 0)),
                  pl.BlockSpec((k, m), lambda i: (0, 0))],
        out_specs=pl.BlockSpec((MATMUL_ROWS, m), lambda i: (i, 0)),
        out_shape=jax.ShapeDtypeStruct((n, m), jnp.float32),
        compiler_params=pltpu.CompilerParams(
            dimension_semantics=("arbitrary",),
            vmem_limit_bytes=48 * 1024 * 1024),
        name=name,
    )(x, w_bf16)


ATTN_BLOCK = 128
ATTN_WINDOW = 256
ATTN_HALF = 64
ATTN_STAGE_ROWS = 256


def _rope_tables(s):
    pos = jnp.arange(s, dtype=jnp.float32)
    inv_freq = ROPE_THETA ** (-jnp.arange(0, HEAD_DIM, 2, dtype=jnp.float32) / HEAD_DIM)
    ang = pos[:, None] * inv_freq[None, :]
    return jnp.cos(ang), jnp.sin(ang)


def _strided_rows(start, size, stride):
    return pl.ds(start, size) if stride == 1 else pl.ds(start, size, stride=stride)


def _attention_kernel(q_ref, k_ref, v_ref, cos_ref, sin_lo_ref, sin_hi_ref, o_ref,
                      qr, kr, q0s, q1s, ks, vs, acc_o, acc_m, acc_l):
    s_len = q_ref.shape[0]
    head0 = lax.broadcasted_iota(jnp.int32, (1, LANES), 1) < HEAD_DIM
    q_i = lax.broadcasted_iota(jnp.int32, (ATTN_BLOCK, ATTN_WINDOW), 0)
    k_j = lax.broadcasted_iota(jnp.int32, (ATTN_BLOCK, ATTN_WINDOW), 1)
    rel = q_i - k_j
    last = len(DILATED_PATTERNS) - 1

    def rotate(j, carry):
        rows = pl.ds(pl.multiple_of(j * ATTN_STAGE_ROWS, ATTN_STAGE_ROWS), ATTN_STAGE_ROWS)
        cos = cos_ref[rows, :]
        sin_lo = sin_lo_ref[rows, :]
        sin_hi = sin_hi_ref[rows, :]

        def rope(t):
            return (t * cos + pltpu.roll(t, LANES - HEAD_DIM // 2, 1) * sin_lo
                    + pltpu.roll(t, HEAD_DIM // 2, 1) * sin_hi)

        qr[rows, :] = rope(q_ref[rows, :]) * (HEAD_DIM ** -0.5)
        kr[rows, :] = rope(k_ref[rows, :])
        return carry

    lax.fori_loop(0, s_len // ATTN_STAGE_ROWS, rotate, 0, unroll=16)

    for pi, (window, dil) in enumerate(DILATED_PATTERNS):
        assert window // (2 * dil) == ATTN_HALF
        row_len = s_len // dil
        assert row_len % ATTN_WINDOW == 0 and row_len % ATTN_STAGE_ROWS == 0

        def stage(j, carry, dil=dil, row_len=row_len):
            f0 = pl.multiple_of(j * ATTN_STAGE_ROWS, ATTN_STAGE_ROWS)
            r = f0 // row_len
            src = _strided_rows((f0 - r * row_len) * dil + r, ATTN_STAGE_ROWS, dil)
            dst = pl.ds(f0, ATTN_STAGE_ROWS)
            qv = qr[src, :]
            q0s[dst, :] = jnp.where(head0, qv, 0.0).astype(jnp.bfloat16)
            q1s[dst, :] = jnp.where(head0, 0.0, qv).astype(jnp.bfloat16)
            ks[dst, :] = kr[src, :].astype(jnp.bfloat16)
            vs[dst, :] = v_ref[src, :].astype(jnp.bfloat16)
            return carry

        lax.fori_loop(0, s_len // ATTN_STAGE_ROWS, stage, 0, unroll=16)

        def block(bi, carry, pi=pi, dil=dil, row_len=row_len):
            f0 = pl.multiple_of(bi * ATTN_BLOCK, ATTN_BLOCK)
            r = f0 // row_len
            row_start = r * row_len
            win = jnp.clip(f0 - ATTN_HALF, row_start, row_start + row_len - ATTN_WINDOW)
            win = pl.multiple_of(win, ATTN_HALF)
            valid = jnp.abs(rel + (f0 - win)) <= ATTN_HALF
            kw = ks[pl.ds(win, ATTN_WINDOW), :]
            vw = vs[pl.ds(win, ATTN_WINDOW), :]
            parts = []
            for qs_ref in (q0s, q1s):
                qb = qs_ref[pl.ds(f0, ATTN_BLOCK), :]
                sc = lax.dot_general(qb, kw, (((1,), (1,)), ((), ())),
                                     preferred_element_type=jnp.float32)
                sc = jnp.where(valid, sc, -jnp.inf)
                m = jnp.max(sc, axis=-1, keepdims=True)
                p = jnp.exp(sc - m)
                den = jnp.sum(p, axis=-1, keepdims=True)
                num = jnp.dot(p.astype(jnp.bfloat16), vw, preferred_element_type=jnp.float32)
                parts.append((num, m, den))
            num = jnp.where(head0, parts[0][0], parts[1][0])
            m = jnp.where(head0, parts[0][1], parts[1][1])
            den = jnp.where(head0, parts[0][2], parts[1][2])
            rows = _strided_rows((f0 - row_start) * dil + r, ATTN_BLOCK, dil)
            if pi == 0:
                acc_o[rows, :] = num
                acc_m[rows, :] = m
                acc_l[rows, :] = den
            else:
                m_old = acc_m[rows, :]
                m_new = jnp.maximum(m_old, m)
                w_old = jnp.exp(m_old - m_new)
                w_cur = jnp.exp(m - m_new)
                num = acc_o[rows, :] * w_old + num * w_cur
                den = acc_l[rows, :] * w_old + den * w_cur
                if pi == last:
                    o_ref[rows, :] = num / den
                else:
                    acc_o[rows, :] = num
                    acc_m[rows, :] = m_new
                    acc_l[rows, :] = den
            return carry

        lax.fori_loop(0, s_len // ATTN_BLOCK, block, 0, unroll=32)


def _attention(proj):
    b, s, _ = proj.shape
    cos, sin = _rope_tables(s)
    reps = LANES // (HEAD_DIM // 2)
    first_half = (jnp.arange(LANES) % HEAD_DIM) < HEAD_DIM // 2
    cos_t = jnp.tile(cos, (1, reps))
    sin_t = jnp.tile(sin, (1, reps))
    sin_lo = jnp.where(first_half, -sin_t, 0.0)
    sin_hi = jnp.where(first_half, 0.0, sin_t)
    pairs = ATTN_WIDTH // LANES

    def cols(off):
        return pl.BlockSpec((None, s, LANES), lambda bi, hp, off=off: (bi, 0, off + hp))

    table = pl.BlockSpec((s, LANES), lambda bi, hp: (0, 0))
    return pl.pallas_call(
        _attention_kernel,
        grid=(b, pairs),
        in_specs=[cols(0), cols(pairs), cols(2 * pairs), table, table, table],
        out_specs=pl.BlockSpec((None, s, LANES), lambda bi, hp: (bi, 0, hp)),
        out_shape=jax.ShapeDtypeStruct((b, s, ATTN_WIDTH), jnp.float32),
        scratch_shapes=[pltpu.VMEM((s, LANES), jnp.float32)] * 2
        + [pltpu.VMEM((s, LANES), jnp.bfloat16)] * 4
        + [pltpu.VMEM((s, LANES), jnp.float32)] * 3,
        compiler_params=pltpu.CompilerParams(
            dimension_semantics=("arbitrary", "arbitrary"),
            vmem_limit_bytes=56 * 1024 * 1024),
        name="dilated_attention",
    )(proj, proj, proj, cos_t, sin_lo, sin_hi)


SSD_HALO = 8
SSD_BATCH = 2
SSD_GROUP_HEADS = SSD_HEADS // SSD_GROUPS
SSD_GROUP_WIDTH = SSD_GROUP_HEADS * SSD_HEAD_DIM


def _softplus(v):
    return jnp.maximum(v, 0.0) + jnp.log(1.0 + jnp.exp(-jnp.abs(v)))


def _expand_heads(mat, first_head):
    t = mat.shape[0]
    low = lax.broadcasted_iota(jnp.int32, (1, LANES), 1) < SSD_HEAD_DIM
    tiles = []
    for j in range(0, SSD_GROUP_HEADS, 2):
        a = jnp.broadcast_to(mat[:, first_head + j:first_head + j + 1], (t, LANES))
        b = jnp.broadcast_to(mat[:, first_head + j + 1:first_head + j + 2], (t, LANES))
        tiles.append(jnp.where(low, a, b))
    return jnp.concatenate(tiles, axis=-1)


def _ssd_kernel(xbc_ref, prev_ref, next_ref, dt_ref, cw_ref, cb_ref, dtb_ref, alog_ref, dsk_ref,
                y_ref, ext, cv, state, *, reverse):
    for bb in range(SSD_BATCH):
        _ssd_chunk(xbc_ref.at[bb], prev_ref.at[bb], next_ref.at[bb], dt_ref.at[bb], cw_ref, cb_ref,
                   dtb_ref, alog_ref, dsk_ref, y_ref.at[bb], ext.at[bb], cv.at[bb], state.at[bb],
                   reverse=reverse)


def _ssd_chunk(xbc_ref, prev_ref, next_ref, dt_ref, cw_ref, cb_ref, dtb_ref, alog_ref, dsk_ref,
               y_ref, ext, cv, state, *, reverse):
    i = pl.program_id(1)
    nc = pl.num_programs(1)
    c = (nc - 1 - i) if reverse else i
    t = SSD_CHUNK
    bf16 = jnp.bfloat16

    @pl.when(i == 0)
    def _():
        state[...] = jnp.zeros_like(state)

    ext[0:SSD_HALO, :] = jnp.where(c > 0, prev_ref[...], 0.0)
    ext[SSD_HALO:SSD_HALO + t, :] = xbc_ref[...]
    ext[SSD_HALO + t:, :] = jnp.where(c < nc - 1, next_ref[...], 0.0)
    for j in range(SSD_CONV_CH // LANES):
        cols = slice(j * LANES, (j + 1) * LANES)
        acc = jnp.broadcast_to(cb_ref[:, cols], (t, LANES))
        for k in range(SSD_CONV):
            acc = acc + cw_ref[k:k + 1, cols] * ext[pl.ds(SSD_HALO - SSD_CONV // 2 + k, t), cols]
        cv[:, cols] = acc * jax.nn.sigmoid(acc)

    dt = _softplus(dt_ref[...] + dtb_ref[...])
    a = dt * (-jnp.exp(alog_ref[...]))
    row = lax.broadcasted_iota(jnp.int32, (t, LANES), 0)
    lane = lax.broadcasted_iota(jnp.int32, (t, LANES), 1)
    cs = a
    shift = 1
    while shift < t:
        cs = cs + jnp.where(row >= shift, pltpu.roll(cs, shift, 0), 0.0)
        shift *= 2
    total = cs[t - 1:t, :]
    e = jnp.where(lane < SSD_HEADS, cs, total - cs + a)
    off = SSD_HEADS if reverse else 0
    edge = e[0:1, :] if reverse else e[t - 1:t, :]
    scale_y = jnp.exp(e)
    scale_s = jnp.exp(edge - e) * dt

    x_off = 0
    b_off = SSD_WIDTH
    c_off = SSD_WIDTH + SSD_GROUPS * SSD_STATE
    low = lax.broadcasted_iota(jnp.int32, (1, LANES), 1) < SSD_HEAD_DIM

    if not reverse:
        e_t = e.T
        dt_t = dt.T
        li = lax.broadcasted_iota(jnp.int32, (t, t), 0)
        si = lax.broadcasted_iota(jnp.int32, (t, t), 1)
        lower = si <= li
        upper = si >= li

    for g in range(SSD_GROUPS):
        bg = cv[:, b_off + g * SSD_STATE:b_off + (g + 1) * SSD_STATE].astype(bf16)
        cg = cv[:, c_off + g * SSD_STATE:c_off + (g + 1) * SSD_STATE].astype(bf16)
        xg = cv[:, x_off + g * SSD_GROUP_WIDTH:x_off + (g + 1) * SSD_GROUP_WIDTH]
        h0 = g * SSD_GROUP_HEADS
        st = state[g]
        y = jnp.dot(cg, st.astype(bf16), preferred_element_type=jnp.float32) \
            * _expand_heads(scale_y, off + h0)
        xw = (xg * _expand_heads(scale_s, off + h0)).astype(bf16)
        new = lax.dot_general(bg, xw, (((0,), (0,)), ((), ())),
                              preferred_element_type=jnp.float32)
        decay = _expand_heads(jnp.broadcast_to(jnp.exp(edge), (SSD_HALO, LANES)), off + h0)
        state[g] = st * decay[0:1, :] + new

        if not reverse:
            cbm = lax.dot_general(cg, bg, (((1,), (1,)), ((), ())),
                                  preferred_element_type=jnp.float32)
            pieces = []
            for j in range(0, SSD_GROUP_HEADS, 2):
                xp = xg[:, j * SSD_HEAD_DIM:(j + 2) * SSD_HEAD_DIM]
                yp = None
                for hh, keep in ((j, low), (j + 1, jnp.logical_not(low))):
                    h = h0 + hh
                    hb = SSD_HEADS + h
                    lf = jnp.exp(jnp.where(lower, e[:, h:h + 1] - e_t[h:h + 1, :], -jnp.inf))
                    lb = jnp.exp(jnp.where(upper, e[:, hb:hb + 1] - e_t[hb:hb + 1, :], -jnp.inf))
                    mh = cbm * (lf * dt_t[h:h + 1, :] + lb * dt_t[hb:hb + 1, :])
                    part = jnp.dot(mh.astype(bf16), jnp.where(keep, xp, 0.0).astype(bf16),
                                   preferred_element_type=jnp.float32)
                    yp = part if yp is None else yp + part
                pieces.append(yp)
            gcols = slice(g * SSD_GROUP_WIDTH, (g + 1) * SSD_GROUP_WIDTH)
            y = y + jnp.concatenate(pieces, axis=-1) + dsk_ref[:, gcols] * xg
        y_ref[:, g * SSD_GROUP_WIDTH:(g + 1) * SSD_GROUP_WIDTH] = y


def _ssd_scan(proj, conv_w, conv_b, dt_bias, a_log, d_skip, reverse):
    b, s, _ = proj.shape
    nc = s // SSD_CHUNK
    z_end = 3 * ATTN_WIDTH + SSD_WIDTH
    xbc_blk = z_end // SSD_CONV_CH
    dt_blk = (z_end + SSD_CONV_CH) // LANES
    assert z_end % SSD_CONV_CH == 0 and (z_end + SSD_CONV_CH) % LANES == 0
    assert b % SSD_BATCH == 0
    halo_per_chunk = SSD_CHUNK // SSD_HALO
    n_halo = s // SSD_HALO

    def chunk(i):
        return (nc - 1 - i) if reverse else i

    cw = jnp.pad(conv_w, ((0, SSD_HALO - SSD_CONV), (0, 0)))
    pad16 = lambda v: jnp.pad(v, (0, LANES - 2 * SSD_HEADS)).reshape(1, LANES)
    vec = lambda width: pl.BlockSpec((1, width), lambda bi, i: (0, 0))
    return pl.pallas_call(
        functools.partial(_ssd_kernel, reverse=reverse),
        grid=(b // SSD_BATCH, nc),
        in_specs=[
            pl.BlockSpec((SSD_BATCH, SSD_CHUNK, SSD_CONV_CH),
                         lambda bi, i: (bi, chunk(i), xbc_blk)),
            pl.BlockSpec((SSD_BATCH, SSD_HALO, SSD_CONV_CH),
                         lambda bi, i: (bi, jnp.maximum(chunk(i) * halo_per_chunk - 1, 0), xbc_blk)),
            pl.BlockSpec((SSD_BATCH, SSD_HALO, SSD_CONV_CH),
                         lambda bi, i: (bi, jnp.minimum((chunk(i) + 1) * halo_per_chunk, n_halo - 1),
                                        xbc_blk)),
            pl.BlockSpec((SSD_BATCH, SSD_CHUNK, LANES), lambda bi, i: (bi, chunk(i), dt_blk)),
            pl.BlockSpec((SSD_HALO, SSD_CONV_CH), lambda bi, i: (0, 0)),
            vec(SSD_CONV_CH), vec(LANES), vec(LANES), vec(SSD_WIDTH),
        ],
        out_specs=pl.BlockSpec((SSD_BATCH, SSD_CHUNK, SSD_WIDTH),
                               lambda bi, i: (bi, chunk(i), 0)),
        out_shape=jax.ShapeDtypeStruct((b, s, SSD_WIDTH), jnp.float32),
        scratch_shapes=[
            pltpu.VMEM((SSD_BATCH, SSD_CHUNK + 2 * SSD_HALO, SSD_CONV_CH), jnp.float32),
            pltpu.VMEM((SSD_BATCH, SSD_CHUNK, SSD_CONV_CH), jnp.float32),
            pltpu.VMEM((SSD_BATCH, SSD_GROUPS, SSD_STATE, SSD_GROUP_WIDTH), jnp.float32),
        ],
        compiler_params=pltpu.CompilerParams(
            dimension_semantics=("arbitrary", "arbitrary"),
            vmem_limit_bytes=32 * 1024 * 1024),
        name="ssd_bwd" if reverse else "ssd_fwd",
    )(proj, proj, proj, proj, cw, conv_b.reshape(1, SSD_CONV_CH), pad16(dt_bias), pad16(a_log),
      jnp.repeat(d_skip, SSD_HEAD_DIM).reshape(1, SSD_WIDTH))


def _ssd_pre_gate(proj, conv_w, conv_b, dt_bias_fwd, a_log_fwd, dt_bias_bwd, a_log_bwd, d_skip):
    dt_bias = jnp.concatenate([dt_bias_fwd, dt_bias_bwd])
    a_log = jnp.concatenate([a_log_fwd, a_log_bwd])
    y_f = _ssd_scan(proj, conv_w, conv_b, dt_bias, a_log, d_skip, reverse=False)
    y_b = _ssd_scan(proj, conv_w, conv_b, dt_bias, a_log, d_skip, reverse=True)
    return y_f, y_b


def _mix_kernel(attn_ref, yf_ref, yb_ref, z_ref, x_ref, anw_ref, snw_ref, wout_ref,
                l1w_ref, l1b_ref, rw_hi_ref, rw_lo_ref, rb_ref, tri_ref,
                h_ref, hp_ref, e_ref, g_ref, r_ref, cnt_ref, carry):
    bf16 = jnp.bfloat16

    @pl.when(pl.program_id(0) == 0)
    def _():
        carry[...] = jnp.zeros_like(carry)

    a = attn_ref[...]
    a = a * lax.rsqrt(jnp.mean(a * a, axis=-1, keepdims=True) + NORM_EPS) * anw_ref[...]
    z = z_ref[...]
    y = (yf_ref[...] + yb_ref[...]) * (z * jax.nn.sigmoid(z))
    groups = []
    for g in range(SSD_GROUPS):
        yg = y[:, g * SSD_GROUP_WIDTH:(g + 1) * SSD_GROUP_WIDTH]
        groups.append(yg * lax.rsqrt(jnp.mean(yg * yg, axis=-1, keepdims=True) + NORM_EPS))
    y = jnp.concatenate(groups, axis=-1) * snw_ref[...]
    mix = (jnp.dot(a.astype(bf16), wout_ref[0:ATTN_WIDTH, :], preferred_element_type=jnp.float32)
           + jnp.dot(y.astype(bf16), wout_ref[ATTN_WIDTH:, :], preferred_element_type=jnp.float32))
    v = DEEPNORM_ALPHA * x_ref[...] + mix
    mu = jnp.mean(v, axis=-1, keepdims=True)
    var = jnp.mean(jnp.square(v - mu), axis=-1, keepdims=True)
    h1 = (v - mu) * lax.rsqrt(var + NORM_EPS) * l1w_ref[...] + l1b_ref[...]
    for c in range(SLAB):
        h_ref[pl.ds(c, MIX_ROWS, stride=SLAB), :] = h1[:, c * LANES:(c + 1) * LANES]
    _store_packed(hp_ref, MIX_ROWS, h1)

    h_hi = h1.astype(bf16)
    h_lo = (h1 - h_hi.astype(jnp.float32)).astype(bf16)
    logits = (jnp.dot(h_hi, rw_hi_ref[...], preferred_element_type=jnp.float32)
              + jnp.dot(h_lo, rw_hi_ref[...], preferred_element_type=jnp.float32)
              + jnp.dot(h_hi, rw_lo_ref[...], preferred_element_type=jnp.float32)
              + rb_ref[...])
    lane = lax.broadcasted_iota(jnp.int32, (MIX_ROWS, LANES), 1)
    lane_f = lane.astype(jnp.float32)
    cur = jnp.where(lane < N_EXPERTS, logits, -jnp.inf)
    vals, idxs = [], []
    for _ in range(TOP_K):
        m = jnp.max(cur, axis=-1, keepdims=True)
        idx = jnp.min(jnp.where(cur == m, lane_f, float(LANES)), axis=-1, keepdims=True)
        vals.append(m)
        idxs.append(idx)
        cur = jnp.where(lane_f == idx, -jnp.inf, cur)
    probs = [jnp.exp(val - vals[0]) for val in vals]
    den = probs[0]
    for p in probs[1:]:
        den = den + p

    sel = jnp.zeros((MIX_ROWS, LANES), jnp.float32)
    for k in range(TOP_K):
        sel = sel + jnp.where(lane_f == idxs[k], 1.0, 0.0)
    rank_all = jnp.dot(tri_ref[...], sel.astype(bf16), preferred_element_type=jnp.float32) \
        + carry[...]
    e_out = jnp.zeros((MIX_ROWS, LANES), jnp.float32)
    g_out = jnp.zeros((MIX_ROWS, LANES), jnp.float32)
    r_out = jnp.zeros((MIX_ROWS, LANES), jnp.float32)
    for k in range(TOP_K):
        rank_k = jnp.sum(jnp.where(lane_f == idxs[k], rank_all, 0.0), axis=-1, keepdims=True)
        e_out = jnp.where(lane == k, idxs[k], e_out)
        g_out = jnp.where(lane == k, probs[k] / den, g_out)
        r_out = jnp.where(lane == k, rank_k, r_out)
    e_ref[...] = e_out.astype(jnp.int32)
    g_ref[...] = g_out
    r_ref[...] = r_out.astype(jnp.int32)
    carry[...] = carry[...] + jnp.sum(sel, axis=0, keepdims=True)
    cnt_ref[...] = jnp.broadcast_to(carry[...], cnt_ref.shape).astype(jnp.int32)


def _mix_ln_router(attn, y_f, y_b, proj, x, attn_norm_w, ssd_norm_w, w_out, ln1_w, ln1_b,
                   router_w, router_b):
    n, d = x.shape
    z_blk = (3 * ATTN_WIDTH) // SSD_WIDTH
    rw = jnp.pad(router_w, ((0, 0), (0, LANES - N_EXPERTS)))
    rw_hi = rw.astype(jnp.bfloat16)
    rw_lo = (rw - rw_hi.astype(jnp.float32)).astype(jnp.bfloat16)
    rb = jnp.pad(router_b, (0, LANES - N_EXPERTS)).reshape(1, LANES)
    tri = (jnp.arange(MIX_ROWS)[None, :] < jnp.arange(MIX_ROWS)[:, None]).astype(jnp.bfloat16)
    half = pl.BlockSpec((MIX_ROWS, ATTN_WIDTH), lambda i: (i, 0))
    vec = lambda width: pl.BlockSpec((1, width), lambda i: (0, 0))
    full = lambda r, c: pl.BlockSpec((r, c), lambda i: (0, 0))
    rows128 = pl.BlockSpec((MIX_ROWS, LANES), lambda i: (i, 0))
    return pl.pallas_call(
        _mix_kernel,
        grid=(n // MIX_ROWS,),
        in_specs=[half, half, half,
                  pl.BlockSpec((MIX_ROWS, SSD_WIDTH), lambda i: (i, z_blk)),
                  pl.BlockSpec((MIX_ROWS, d), lambda i: (i, 0)),
                  vec(ATTN_WIDTH), vec(SSD_WIDTH), full(d, d), vec(d), vec(d),
                  full(d, LANES), full(d, LANES), vec(LANES), full(MIX_ROWS, MIX_ROWS)],
        out_specs=[pl.BlockSpec((MIX_ROWS * SLAB, LANES), lambda i: (i, 0)),
                   pl.BlockSpec((MIX_ROWS * PACK, LANES), lambda i: (i, 0)),
                   rows128, rows128, rows128,
                   pl.BlockSpec((SLAB, LANES), lambda i: (0, 0))],
        out_shape=[jax.ShapeDtypeStruct((n * SLAB, LANES), jnp.float32),
                   jax.ShapeDtypeStruct((n * PACK, LANES), jnp.uint32),
                   jax.ShapeDtypeStruct((n, LANES), jnp.int32),
                   jax.ShapeDtypeStruct((n, LANES), jnp.float32),
                   jax.ShapeDtypeStruct((n, LANES), jnp.int32),
                   jax.ShapeDtypeStruct((SLAB, LANES), jnp.int32)],
        scratch_shapes=[pltpu.VMEM((1, LANES), jnp.float32)],
        compiler_params=pltpu.CompilerParams(
            dimension_semantics=("arbitrary",),
            vmem_limit_bytes=48 * 1024 * 1024),
        name="mix_ln_router",
    )(attn, y_f, y_b, proj, x, attn_norm_w.reshape(1, ATTN_WIDTH), ssd_norm_w.reshape(1, SSD_WIDTH),
      w_out.astype(jnp.bfloat16), ln1_w.reshape(1, d), ln1_b.reshape(1, d), rw_hi, rw_lo, rb, tri)


def _slab_of(row):
    return pl.ds(pl.multiple_of(row * PACK, PACK), PACK)


def _dispatch_kernel(last_ref, has_ref, nt_ref, pos_ref, h_ref, xs_hbm, zbuf, sem, zsem):
    i = pl.program_id(0)
    t_max = xs_hbm.shape[0] // (MOE_TILE * PACK)

    @pl.when(i == 0)
    def _zero_fill():
        zbuf[...] = jnp.zeros_like(zbuf)

        def fill(tile):
            return pltpu.make_async_copy(
                zbuf, xs_hbm.at[pl.ds(pl.multiple_of(tile * (MOE_TILE * PACK), MOE_TILE * PACK),
                                      MOE_TILE * PACK), :], zsem)

        for e in range(N_EXPERTS):
            @pl.when(has_ref[e] > 0)
            def _(e=e):
                fill(last_ref[e]).start()
        for j in range(N_EXPERTS + 1):
            @pl.when(nt_ref[0] + j < t_max)
            def _(j=j):
                fill(nt_ref[0] + j).start()
        for e in range(N_EXPERTS):
            @pl.when(has_ref[e] > 0)
            def _(e=e):
                fill(last_ref[e]).wait()
        for j in range(N_EXPERTS + 1):
            @pl.when(nt_ref[0] + j < t_max)
            def _(j=j):
                fill(nt_ref[0] + j).wait()

    def copy(k, r):
        return pltpu.make_async_copy(h_ref.at[_slab_of(r), :],
                                     xs_hbm.at[_slab_of(pos_ref[0, 0, k * MIX_ROWS + r]), :], sem)

    def start(r, c):
        for k in range(TOP_K):
            copy(k, r).start(priority=k % 2)
        return c

    def wait(r, c):
        for k in range(TOP_K):
            copy(k, r).wait()
        return c

    lax.fori_loop(0, MIX_ROWS, start, 0, unroll=4)
    lax.fori_loop(0, MIX_ROWS, wait, 0, unroll=4)


def _dispatch(h1_packed, pos_tab, last_tile, has_tile, n_tiles, n_rows):
    nb = pos_tab.shape[0]
    grid_spec = pltpu.PrefetchScalarGridSpec(
        num_scalar_prefetch=3,
        grid=(nb,),
        in_specs=[pl.BlockSpec((1, 1, TOP_K * MIX_ROWS), lambda i, la, ha, nt: (i, 0, 0),
                               memory_space=pltpu.SMEM),
                  pl.BlockSpec((MIX_ROWS * PACK, LANES), lambda i, la, ha, nt: (i, 0))],
        out_specs=pl.BlockSpec(memory_space=pl.ANY),
        scratch_shapes=[pltpu.VMEM((MOE_TILE * PACK, LANES), jnp.uint32),
                        pltpu.SemaphoreType.DMA(()),
                        pltpu.SemaphoreType.DMA(())],
    )
    return pl.pallas_call(
        _dispatch_kernel,
        grid_spec=grid_spec,
        out_shape=jax.ShapeDtypeStruct((n_rows * PACK, LANES), jnp.uint32),
        compiler_params=pltpu.CompilerParams(
            dimension_semantics=("arbitrary",),
            vmem_limit_bytes=32 * 1024 * 1024),
        name="dispatch",
    )(last_tile, has_tile, n_tiles, pos_tab, h1_packed)


def _moe_kernel(te_ref, nt_ref, dst_prv_ref,
                x_ref, wg_ref, wu_ref, wd_ref, bg_ref, bu_ref, bd_ref,
                y_hbm,
                ybuf, xb_s, act_s, wgu_bf, wd_bf, ssem):
    i = pl.program_id(0)
    n_tiles = nt_ref[0]
    slot = i % 2
    other = 1 - slot
    n_pairs = y_hbm.shape[0] // PACK - 2 * MOE_TILE
    bf16 = jnp.bfloat16

    def scatter_copy(dst, row, s):
        return pltpu.make_async_copy(ybuf.at[s, _slab_of(row), :],
                                     y_hbm.at[_slab_of(dst), :], ssem.at[s])

    def start_scatter_prev(r, prio=0):
        dst = jnp.where(i == 0, n_pairs + MOE_TILE + r, dst_prv_ref[0, 0, r])
        scatter_copy(dst, r, other).start(priority=prio)

    def wait_scatter(s):
        def body(r, c):
            scatter_copy(0, r, s).wait()
            return c
        lax.fori_loop(0, MOE_TILE, body, 0, unroll=8)

    @pl.when(i == 0)
    def _first():
        ybuf[...] = jnp.zeros_like(ybuf)
        for half in range(2):
            cp = pltpu.make_async_copy(
                ybuf.at[0],
                y_hbm.at[pl.ds((n_pairs + half * MOE_TILE) * PACK, MOE_TILE * PACK), :],
                ssem.at[half])
            cp.start()
            cp.wait()

    @pl.when((i >= 1) & (i <= n_tiles))
    def _arrive():
        wait_scatter(slot)

    @pl.when(i < n_tiles)
    def _compute():
        e = te_ref[i]
        prev_e = te_ref[jnp.maximum(i - 1, 0)]

        @pl.when((i == 0) | (e != prev_e))
        def _():
            for c in range(D_MODEL // FFN_CHUNK):
                cols = slice(c * FFN_CHUNK, (c + 1) * FFN_CHUNK)
                wgu_bf[:, (2 * c) * FFN_CHUNK:(2 * c + 1) * FFN_CHUNK] = wg_ref[0, :, cols].astype(bf16)
                wgu_bf[:, (2 * c + 1) * FFN_CHUNK:(2 * c + 2) * FFN_CHUNK] = \
                    wu_ref[0, :, cols].astype(bf16)
            wd_bf[...] = wd_ref[0].astype(bf16)

        for c in range(PACK):
            lo, hi = _unpack_pair(x_ref[pl.ds(c, MOE_TILE, stride=PACK), :])
            xb_s[:, (2 * c) * LANES:(2 * c + 1) * LANES] = lo.astype(bf16)
            xb_s[:, (2 * c + 1) * LANES:(2 * c + 2) * LANES] = hi.astype(bf16)

        batch = MOE_TILE // (D_MODEL // FFN_CHUNK)
        for c in range(D_MODEL // FFN_CHUNK):
            cols = slice(c * FFN_CHUNK, (c + 1) * FFN_CHUNK)
            gu = jnp.dot(xb_s[...], wgu_bf[:, (2 * c) * FFN_CHUNK:(2 * c + 2) * FFN_CHUNK],
                         preferred_element_type=jnp.float32)
            g = gu[:, :FFN_CHUNK] + bg_ref[pl.ds(e, 1), cols]
            u = gu[:, FFN_CHUNK:] + bu_ref[pl.ds(e, 1), cols]
            for r in range(c * batch, (c + 1) * batch):
                start_scatter_prev(r, r % 2)
            g = jnp.minimum(g, SWIGLU_LIMIT)
            u = jnp.clip(u, -SWIGLU_LIMIT, SWIGLU_LIMIT)
            act_s[:, cols] = ((u + 1.0) * g * jax.nn.sigmoid(SWIGLU_ALPHA * g)).astype(bf16)

        for c in range(D_MODEL // FFN_CHUNK):
            cols = slice(c * FFN_CHUNK, (c + 1) * FFN_CHUNK)
            y = jnp.dot(act_s[...], wd_bf[:, cols], preferred_element_type=jnp.float32)
            y = y + bd_ref[pl.ds(e, 1), cols]
            for j in range(FFN_CHUNK // (2 * LANES)):
                ybuf[slot, pl.ds(c * (FFN_CHUNK // (2 * LANES)) + j, MOE_TILE, stride=PACK), :] = \
                    _pack_pair(y[:, (2 * j) * LANES:(2 * j + 1) * LANES],
                               y[:, (2 * j + 1) * LANES:(2 * j + 2) * LANES])

    @pl.when(i == n_tiles)
    def _drain():
        def body(r, c):
            start_scatter_prev(r)
            return c
        lax.fori_loop(0, MOE_TILE, body, 0, unroll=8)
        wait_scatter(other)


def _moe_ffn(xs, tile_e, n_tiles, dst_row, w_gate, b_gate, w_up, b_up, w_down, b_down,
             n_out_rows):
    t_max = tile_e.shape[0]
    d = D_MODEL
    w_spec = pl.BlockSpec((1, d, d), lambda i, te, nt: (te[i], 0, 0))
    b_spec = pl.BlockSpec((N_EXPERTS, d), lambda i, te, nt: (0, 0))
    grid_spec = pltpu.PrefetchScalarGridSpec(
        num_scalar_prefetch=2,
        grid=(t_max,),
        in_specs=[pl.BlockSpec((1, 1, MOE_TILE), lambda i, te, nt: (jnp.maximum(i - 1, 0), 0, 0),
                               memory_space=pltpu.SMEM),
                  pl.BlockSpec((MOE_TILE * PACK, LANES),
                               lambda i, te, nt: (jnp.minimum(i, nt[0] - 1), 0)),
                  w_spec, w_spec, w_spec, b_spec, b_spec, b_spec],
        out_specs=pl.BlockSpec(memory_space=pl.ANY),
        scratch_shapes=[
            pltpu.VMEM((2, MOE_TILE * PACK, LANES), jnp.uint32),
            pltpu.VMEM((MOE_TILE, d), jnp.bfloat16),
            pltpu.VMEM((MOE_TILE, d), jnp.bfloat16),
            pltpu.VMEM((d, 2 * d), jnp.bfloat16),
            pltpu.VMEM((d, d), jnp.bfloat16),
            pltpu.SemaphoreType.DMA((2,)),
        ],
    )
    return pl.pallas_call(
        _moe_kernel,
        grid_spec=grid_spec,
        out_shape=jax.ShapeDtypeStruct((n_out_rows * PACK, LANES), jnp.uint32),
        compiler_params=pltpu.CompilerParams(
            dimension_semantics=("arbitrary",),
            vmem_limit_bytes=V7X_VMEM_BYTES - 6 * 1024 * 1024),
        name="moe_ffn",
    )(tile_e, n_tiles, dst_row, xs, w_gate, w_up, w_down, b_gate, b_up, b_down)


def _combine_ln_kernel(y0_ref, y1_ref, y2_ref, y3_ref, h_ref, g_ref, w_ref, b_ref, o_ref):
    def plane_rows(ref):
        tiles = []
        for c in range(PACK):
            tiles.extend(_unpack_pair(ref[pl.ds(c, LN_ROWS, stride=PACK), :]))
        return jnp.concatenate(tiles, axis=-1)

    g = g_ref[...]
    ffn = (g[:, 0:1] * plane_rows(y0_ref) + g[:, 1:2] * plane_rows(y1_ref)
           + g[:, 2:3] * plane_rows(y2_ref) + g[:, 3:4] * plane_rows(y3_ref))
    v = DEEPNORM_ALPHA * _slab_rows(h_ref, LN_ROWS) + ffn
    mu = jnp.mean(v, axis=-1, keepdims=True)
    var = jnp.mean(jnp.square(v - mu), axis=-1, keepdims=True)
    o_ref[...] = (v - mu) * lax.rsqrt(var + NORM_EPS) * w_ref[...] + b_ref[...]


def _combine_ln(y_planes, h1_slab, gates, ln_w, ln_b):
    n = gates.shape[0]
    d = D_MODEL
    nb = n // LN_ROWS

    def plane(k):
        return pl.BlockSpec((LN_ROWS * PACK, LANES), lambda i, k=k: (k * nb + i, 0))

    vec = pl.BlockSpec((1, d), lambda i: (0, 0))
    return pl.pallas_call(
        _combine_ln_kernel,
        grid=(nb,),
        in_specs=[plane(0), plane(1), plane(2), plane(3),
                  pl.BlockSpec((LN_ROWS * SLAB, LANES), lambda i: (i, 0)),
                  pl.BlockSpec((LN_ROWS, LANES), lambda i: (i, 0)), vec, vec],
        out_specs=pl.BlockSpec((LN_ROWS, d), lambda i: (i, 0)),
        out_shape=jax.ShapeDtypeStruct((n, d), jnp.float32),
        compiler_params=pltpu.CompilerParams(
            dimension_semantics=("arbitrary",),
            vmem_limit_bytes=48 * 1024 * 1024),
        name="combine_ln",
    )(y_planes, y_planes, y_planes, y_planes, h1_slab, gates,
      ln_w.reshape(1, d), ln_b.reshape(1, d))


def _routing_tables(top_e, rank, counts, n_tok):
    n_pairs = TOP_K * n_tok
    t_max = n_pairs // MOE_TILE + N_EXPERTS + 1
    n_rows = t_max * MOE_TILE
    tiles_per_e = (counts + MOE_TILE - 1) // MOE_TILE
    tile_end = jnp.cumsum(tiles_per_e)
    tile_start = tile_end - tiles_per_e
    n_tiles = tile_end[-1]
    tile_ids = jnp.arange(t_max, dtype=jnp.int32)
    tile_e = jnp.sum((tile_end[None, :] <= jnp.minimum(tile_ids, n_tiles - 1)[:, None])
                     .astype(jnp.int32), axis=1)
    tile_e = jnp.minimum(tile_e, N_EXPERTS - 1).astype(jnp.int32)

    base = (tile_start * MOE_TILE).astype(jnp.int32)
    experts = jnp.arange(N_EXPERTS, dtype=jnp.int32)
    pos = jnp.sum(jnp.where(top_e[:, :, None] == experts[None, None, :], base[None, None, :], 0),
                  axis=-1) + rank
    assert N_EXPERTS * n_pairs < 2 ** 31
    flat = jnp.arange(n_pairs, dtype=jnp.int32).reshape(n_tok, TOP_K)
    by_row = jnp.sort((top_e * n_pairs + flat).reshape(-1)) % n_pairs
    pairs_by_row = (by_row % TOP_K) * n_tok + by_row // TOP_K
    first = (jnp.cumsum(counts) - counts).astype(jnp.int32)
    lane = jnp.arange(MOE_TILE, dtype=jnp.int32)[None, :]
    row_in_e = (tile_ids - tile_start[tile_e])[:, None] * MOE_TILE + lane
    valid = (row_in_e < counts[tile_e][:, None]) & (tile_ids < n_tiles)[:, None]
    owner = pairs_by_row[jnp.clip(first[tile_e][:, None] + row_in_e, 0, n_pairs - 1)]
    spare = n_pairs + (tile_ids % 2)[:, None] * MOE_TILE + lane
    dst_row = jnp.where(valid, owner, spare).astype(jnp.int32).reshape(t_max, 1, MOE_TILE)

    nb = n_tok // MIX_ROWS
    pos_tab = pos.reshape(nb, MIX_ROWS, TOP_K).transpose(0, 2, 1).reshape(nb, 1, TOP_K * MIX_ROWS)
    last_tile = jnp.maximum(tile_end - 1, 0).astype(jnp.int32)
    has_tile = (tiles_per_e > 0).astype(jnp.int32)
    return (tile_e, n_tiles.reshape(1).astype(jnp.int32), pos_tab, dst_row, last_tile, has_tile,
            n_rows)


def kernel(x, w_in, attn_norm_w, conv_w, conv_b, dt_bias_fwd, a_log_fwd, dt_bias_bwd,
           a_log_bwd, d_skip, ssd_norm_w, w_out, ln1_w, ln1_b, router_w, router_b,
           w_gate, b_gate, w_up, b_up, w_down, b_down, ln2_w, ln2_b):
    b, s, d = x.shape
    n = b * s
    xf = x.reshape(n, d)
    in_cols = w_in.shape[-1]
    in_cols_pad = -(-in_cols // LANES) * LANES

    w_in_b = jnp.pad(w_in[0], ((0, 0), (0, in_cols_pad - in_cols))).astype(jnp.bfloat16)
    proj = _rows_matmul(xf, w_in_b, "in_proj")
    proj3 = proj.reshape(b, s, in_cols_pad)
    attn = _attention(proj3)
    y_f, y_b = _ssd_pre_gate(proj3, conv_w[0], conv_b[0], dt_bias_fwd[0], a_log_fwd[0],
                             dt_bias_bwd[0], a_log_bwd[0], d_skip[0])
    h1_slab, h1_packed, top_e, gates, rank, counts = _mix_ln_router(
        attn.reshape(n, ATTN_WIDTH), y_f.reshape(n, SSD_WIDTH), y_b.reshape(n, SSD_WIDTH),
        proj, xf, attn_norm_w[0], ssd_norm_w[0], w_out[0], ln1_w[0], ln1_b[0],
        router_w[0], router_b[0])

    tile_e, n_tiles, pos_tab, dst_row, last_tile, has_tile, n_rows = _routing_tables(
        top_e[:, :TOP_K], rank[:, :TOP_K], counts[0, :N_EXPERTS], n)
    xs = _dispatch(h1_packed, pos_tab, last_tile, has_tile, n_tiles, n_rows)
    n_out_rows = TOP_K * n + 2 * MOE_TILE
    y_planes = _moe_ffn(xs, tile_e, n_tiles, dst_row,
                        w_gate[0], b_gate[0], w_up[0], b_up[0], w_down[0], b_down[0],
                        n_out_rows)
    out = _combine_ln(y_planes, h1_slab, gates, ln2_w[0], ln2_b[0])
    return out.reshape(b, s, d)
```
